```python
import jax, jax.numpy as jnp
from jax import lax
import numpy as np


D_MODEL = 2048
BATCH = 1
SEQ = 16384
DEPTH = 2
DEC_BATCH = 4
DEC_SEQ = 8192
PAST_LEN = 128

GLA_HEADS = 4
GLA_WIDTH = D_MODEL // 2
GLA_DV = GLA_WIDTH // GLA_HEADS
GLA_DK = GLA_DV // 2
GLA_KWIDTH = GLA_HEADS * GLA_DK
GLA_RANK = 16
GLA_TAU = 16.0
HG_DIM = 128
HG_WIDTH = D_MODEL - GLA_WIDTH
HG_HEADS = HG_WIDTH // HG_DIM
MIX_WIDTH = GLA_WIDTH + HG_WIDTH
D_FF = 4 * D_MODEL
CHUNK = 64
LN_EPS = 1e-5
RMS_EPS = 1e-6
ALPHA = (2.0 * DEPTH) ** 0.25
BETA = (8.0 * DEPTH) ** -0.25
SPLIT_WIDTHS = (GLA_KWIDTH, GLA_KWIDTH, GLA_WIDTH, GLA_WIDTH, 2 * GLA_RANK,
                HG_WIDTH, HG_WIDTH, HG_WIDTH, HG_WIDTH, HG_WIDTH)
IN_WIDTH = sum(SPLIT_WIDTHS)
SPLIT_IDX = tuple(sum(SPLIT_WIDTHS[:i + 1]) for i in range(len(SPLIT_WIDTHS) - 1))

kernel_name = 'hymba_gla_hgrn2_bidir_encoder'


def _heads(a, n_heads):
    B, T, W = a.shape
    return a.reshape(B, T, n_heads, W // n_heads).transpose(0, 2, 1, 3)


def _merge(a):
    B, H, T, d = a.shape
    return a.transpose(0, 2, 1, 3).reshape(B, T, H * d)


def _rmsnorm(a, g):
    a = a.astype(jnp.float32)
    return a * lax.rsqrt(jnp.mean(a * a, axis=-1, keepdims=True) + RMS_EPS) * g.astype(jnp.float32)


def _layernorm(a, g, b):
    a = a.astype(jnp.float32)
    mu = jnp.mean(a, axis=-1, keepdims=True)
    var = jnp.mean(jnp.square(a - mu), axis=-1, keepdims=True)
    return (a - mu) * lax.rsqrt(var + LN_EPS) * g.astype(jnp.float32) + b.astype(jnp.float32)


def _chunk_gated_linear(q, k, v, g):
    B, H, T, dk = q.shape
    dv = v.shape[-1]
    n = T // CHUNK

    def to_chunks(a):
        return a.reshape(B, H, n, CHUNK, a.shape[-1]).transpose(2, 0, 1, 3, 4)

    mask = jnp.tril(jnp.ones((CHUNK, CHUNK), dtype=bool))[:, :, None]

    def step(S, inp):
        qc, kc, vc, gc = inp
        b = jnp.cumsum(gc, axis=2)
        o_inter = jnp.einsum('bhid,bhde->bhie', qc * jnp.exp(b), S)
        diff = b[:, :, :, None, :] - b[:, :, None, :, :]
        decay = jnp.where(mask, jnp.exp(jnp.where(mask, diff, 0.0)), 0.0)
        scores = jnp.einsum('bhid,bhjd,bhijd->bhij', qc, kc, decay)
        o = o_inter + jnp.einsum('bhij,bhje->bhie', scores, vc)
        b_last = b[:, :, -1:, :]
        S = jnp.exp(b_last[:, :, 0, :])[..., None] * S + jnp.einsum(
            'bhjd,bhje->bhde', kc * jnp.exp(b_last - b), vc)
        return S, o

    S0 = jnp.zeros((B, H, dk, dv), jnp.float32)
    _, o = lax.scan(step, S0, (to_chunks(q), to_chunks(k), to_chunks(v), to_chunks(g)))
    return o.transpose(1, 2, 0, 3, 4).reshape(B, H, T, dv)


def _bidirectional(q, k_f, k_b, v, g_f, g_b):
    flip = lambda a: jnp.flip(a, axis=2)
    fwd = _chunk_gated_linear(q, k_f, v, g_f)
    bwd = flip(_chunk_gated_linear(flip(q), flip(k_b), flip(v), flip(g_b)))
    return fwd + bwd


def _hgrn_gate(z, lb):
    log_f = jnp.logaddexp(jnp.log(lb), jnp.log1p(-lb) + jax.nn.log_sigmoid(z))
    one_minus_f = (1.0 - lb) * jax.nn.sigmoid(-z)
    return log_f, one_minus_f


def _lower_bounds(p):
    c = jnp.cumsum(jax.nn.softmax(p.astype(jnp.float32), axis=0), axis=0)
    return c - c[0:1]


def _mixer(x, w_in, w_lr2, b_lr, gla_g, hg_g, lb_f, lb_b, w_out):
    z = jnp.einsum('btd,de->bte', x, w_in).astype(jnp.float32)
    gq, gk, gv, gog, glr, hq, hff, hfb, hi, hog = jnp.split(z, SPLIT_IDX, axis=-1)
    w_lr2 = w_lr2.astype(jnp.float32)
    b_lr = b_lr.astype(jnp.float32)
    ga_f = jax.nn.log_sigmoid(glr[..., :GLA_RANK] @ w_lr2[0] + b_lr[0]) / GLA_TAU
    ga_b = jax.nn.log_sigmoid(glr[..., GLA_RANK:] @ w_lr2[1] + b_lr[1]) / GLA_TAU
    q = _heads(gq * (GLA_DK ** -0.5), GLA_HEADS)
    k = _heads(gk, GLA_HEADS)
    v = _heads(gv, GLA_HEADS)
    o_gla = _bidirectional(q, k, k, v, _heads(ga_f, GLA_HEADS), _heads(ga_b, GLA_HEADS))
    o_gla = _merge(_rmsnorm(o_gla, gla_g)) * jax.nn.silu(gog)
    logf_f, kf = _hgrn_gate(hff, lb_f)
    logf_b, kb = _hgrn_gate(hfb, lb_b)
    o_hg = _bidirectional(_heads(hq, HG_HEADS), _heads(kf, HG_HEADS), _heads(kb, HG_HEADS),
                          _heads(hi, HG_HEADS), _heads(logf_f, HG_HEADS), _heads(logf_b, HG_HEADS))
    o_hg = _merge(_rmsnorm(o_hg, hg_g)) * jax.nn.silu(hog)
    o = jnp.concatenate([o_gla, o_hg], axis=-1)
    return jnp.einsum('bte,ed->btd', o, w_out)


def _mlp(x, w_up, w_down):
    h = jnp.square(jax.nn.relu(jnp.einsum('btd,df->btf', x, w_up)))
    return jnp.einsum('btf,fd->btd', h, w_down)


def _trunk(x, w_in, gla_w_lr2, gla_b_lr, gla_norm_g, hg_norm_g, lbs_f, lbs_b,
           w_out, ln1_g, ln1_b, w_up, w_down, ln2_g, ln2_b):
    dt = x.dtype
    h = x.astype(jnp.float32)
    for l in range(DEPTH):
        m = _mixer(h, w_in[l], gla_w_lr2[l], gla_b_lr[l], gla_norm_g[l], hg_norm_g[l],
                   lbs_f[l], lbs_b[l], w_out[l])
        h = _layernorm(ALPHA * h + m, ln1_g[l], ln1_b[l])
        f = _mlp(h, w_up[l], w_down[l])
        h = _layernorm(ALPHA * h + f, ln2_g[l], ln2_b[l])
    return h.astype(dt)


def setup_inputs(seed: int = 0) -> dict:
    key = jax.random.key(seed)
    ks = jax.random.split(key, 16)
    f32 = jnp.float32
    nrm = lambda k, s, sc: jax.random.normal(k, s, f32) * sc
    return {
        'x_prompt': nrm(ks[0], (BATCH, SEQ, D_MODEL), 1.0),
        'x_sample': nrm(ks[1], (DEC_BATCH, DEC_SEQ, D_MODEL), 1.0),
        'w_in': nrm(ks[2], (DEPTH, D_MODEL, IN_WIDTH), D_MODEL ** -0.5),
        'gla_w_lr2': nrm(ks[3], (DEPTH, 2, GLA_RANK, GLA_KWIDTH), GLA_RANK ** -0.5),
        'gla_b_lr': nrm(ks[4], (DEPTH, 2, GLA_KWIDTH), 0.1),
        'gla_norm_g': 1.0 + nrm(ks[5], (DEPTH, GLA_DV), 0.02),
        'hg_norm_g': 1.0 + nrm(ks[6], (DEPTH, HG_DIM), 0.02),
        'lower_bounds': nrm(ks[7], (2, DEPTH, HG_WIDTH), 0.1),
        'w_out': nrm(ks[8], (DEPTH, MIX_WIDTH, D_MODEL), BETA * MIX_WIDTH ** -0.5),
        'ln1_g': 1.0 + nrm(ks[9], (DEPTH, D_MODEL), 0.02),
        'ln1_b': nrm(ks[10], (DEPTH, D_MODEL), 0.02),
        'w_up': nrm(ks[11], (DEPTH, D_MODEL, D_FF), D_MODEL ** -0.5),
        'w_down': nrm(ks[12], (DEPTH, D_FF, D_MODEL), BETA * D_FF ** -0.5),
        'ln2_g': 1.0 + nrm(ks[13], (DEPTH, D_MODEL), 0.02),
        'ln2_b': nrm(ks[14], (DEPTH, D_MODEL), 0.02),
    }


def reference(x_prompt, x_sample, w_in, gla_w_lr2, gla_b_lr, gla_norm_g, hg_norm_g,
              lower_bounds, w_out, ln1_g, ln1_b, w_up, w_down, ln2_g, ln2_b):
    lbs_f = _lower_bounds(lower_bounds[0])
    lbs_b = _lower_bounds(lower_bounds[1])
    y_prompt = _trunk(x_prompt, w_in, gla_w_lr2, gla_b_lr, gla_norm_g, hg_norm_g, lbs_f, lbs_b,
                      w_out, ln1_g, ln1_b, w_up, w_down, ln2_g, ln2_b)
    y_sample = _trunk(x_sample, w_in, gla_w_lr2, gla_b_lr, gla_norm_g, hg_norm_g, lbs_f, lbs_b,
                      w_out, ln1_g, ln1_b, w_up, w_down, ln2_g, ln2_b)
    return (y_prompt, y_sample)
```

```python
import functools

import jax
import jax.numpy as jnp
from jax import lax
from jax.experimental import pallas as pl
from jax.experimental.pallas import tpu as pltpu

F32 = jnp.float32
BF16 = jnp.bfloat16

D_MODEL = 2048
DEPTH = 2
GLA_HEADS = 4
GLA_WIDTH = D_MODEL // 2
GLA_DV = GLA_WIDTH // GLA_HEADS
GLA_DK = GLA_DV // 2
GLA_KWIDTH = GLA_HEADS * GLA_DK
GLA_RANK = 16
GLA_TAU = 16.0
HG_DIM = 128
HG_WIDTH = D_MODEL - GLA_WIDTH
HG_HEADS = HG_WIDTH // HG_DIM
D_FF = 4 * D_MODEL
LN_EPS = 1e-5
RMS_EPS = 1e-6
ALPHA = (2.0 * DEPTH) ** 0.25

LANES = 128
GLR_PAD = LANES
Z_WIDTH = 8192
VMEM_LIMIT = 48 * 1024 * 1024

IN_TN = 512
IN_TM = 1024
GATE_TILE_LO = 4096 // IN_TN
GATE_TILE_HI = 6144 // IN_TN

CHUNK = 128
MAX_HALF_RANGE = 80.0

OUT_TM = 256
MLP_TM = 512
MLP_TF = 512

_NT = (((1,), (1,)), ((), ()))
_TN = (((0,), (0,)), ((), ()))


def _log_sigmoid(x):
    return jnp.minimum(x, 0.0) - jnp.log1p(jnp.exp(-jnp.abs(x)))


def _layernorm(r, g, b):
    mu = jnp.mean(r, axis=-1, keepdims=True)
    d = r - mu
    var = jnp.mean(d * d, axis=-1, keepdims=True)
    return d * lax.rsqrt(var + LN_EPS) * g + b


def _inproj_kernel(x_ref, w_ref, wglr_ref, w2_ref, b2_ref, gc_ref, z_ref, kfb_ref, ga_ref, xb_ref):
    j = pl.program_id(1)

    @pl.when(j == 0)
    def _():
        xb = x_ref[...].astype(BF16)
        xb_ref[...] = xb
        glr = jnp.dot(xb, wglr_ref[...], preferred_element_type=F32)
        a = jnp.dot(glr.astype(BF16), w2_ref[...], preferred_element_type=F32) + b2_ref[...]
        ga_ref[...] = _log_sigmoid(a) * (1.0 / GLA_TAU)

    acc = jnp.dot(xb_ref[...], w_ref[...], preferred_element_type=F32)
    is_gate = jnp.logical_and(j >= GATE_TILE_LO, j < GATE_TILE_HI)

    @pl.when(j == 0)
    def _():
        z_ref[...] = acc * (GLA_DK ** -0.5)

    @pl.when(jnp.logical_and(j > 0, jnp.logical_not(is_gate)))
    def _():
        z_ref[...] = acc

    @pl.when(is_gate)
    def _():
        e = jnp.exp(-jnp.abs(acc))
        ls = jnp.minimum(acc, 0.0) - jnp.log1p(e)
        log_lb = gc_ref[0:1, :]
        x2 = gc_ref[1:2, :] + ls
        z_ref[...] = jnp.maximum(log_lb, x2) + jnp.log1p(jnp.exp(-jnp.abs(log_lb - x2)))
        kfb_ref[...] = gc_ref[2:3, :] * (jnp.where(acc >= 0.0, e, 1.0) / (1.0 + e))


def _inproj(x, w_main, w_glr, w2c, b2c, gate_c):
    n = x.shape[0]
    tm = min(IN_TM, n)
    n_gate = GATE_TILE_HI - GATE_TILE_LO
    gate_idx = lambda i, j: jnp.clip(j - GATE_TILE_LO, 0, n_gate - 1)
    return pl.pallas_call(
        _inproj_kernel,
        grid=(n // tm, Z_WIDTH // IN_TN),
        in_specs=[
            pl.BlockSpec((tm, D_MODEL), lambda i, j: (i, 0)),
            pl.BlockSpec((D_MODEL, IN_TN), lambda i, j: (0, j)),
            pl.BlockSpec((D_MODEL, GLR_PAD), lambda i, j: (0, 0)),
            pl.BlockSpec((GLR_PAD, 2 * GLA_KWIDTH), lambda i, j: (0, 0)),
            pl.BlockSpec((1, 2 * GLA_KWIDTH), lambda i, j: (0, 0)),
            pl.BlockSpec((8, IN_TN), lambda i, j: (0, gate_idx(i, j))),
        ],
        out_specs=[
            pl.BlockSpec((tm, IN_TN), lambda i, j: (i, j)),
            pl.BlockSpec((tm, IN_TN), lambda i, j: (i, gate_idx(i, j))),
            pl.BlockSpec((tm, 2 * GLA_KWIDTH), lambda i, j: (i, 0)),
        ],
        out_shape=[
            jax.ShapeDtypeStruct((n, Z_WIDTH), F32),
            jax.ShapeDtypeStruct((n, 2 * HG_WIDTH), F32),
            jax.ShapeDtypeStruct((n, 2 * GLA_KWIDTH), F32),
        ],
        scratch_shapes=[pltpu.VMEM((tm, D_MODEL), BF16)],
        compiler_params=pltpu.CompilerParams(
            dimension_semantics=("arbitrary", "arbitrary"), vmem_limit_bytes=VMEM_LIMIT),
        name="inproj",
    )(x, w_main, w_glr, w2c, b2c, gate_c)


def _cumsum_rows(tri, g):
    hi = g.astype(BF16)
    lo = (g - hi.astype(F32)).astype(BF16)
    return (jnp.dot(tri, hi, preferred_element_type=F32) + jnp.dot(tri, lo, preferred_element_type=F32))


def _dir_fast(q, k, v, b, tot, st, mask):
    m = 0.5 * tot
    qt = (q * jnp.exp(b - m)).astype(BF16)
    kt = (k * jnp.exp(m - b)).astype(BF16)
    vb = v.astype(BF16)
    s = lax.dot_general(qt, kt, _NT, preferred_element_type=F32)
    s = jnp.where(mask, s, 0.0).astype(BF16)
    o = jnp.dot(s, vb, preferred_element_type=F32)
    ssc = (st * jnp.exp(m)).astype(BF16)
    o = o + lax.dot_general(qt, ssc, _NT, preferred_element_type=F32)
    u = lax.dot_general(vb, kt, _TN, preferred_element_type=F32)
    return o, st * jnp.exp(tot) + u * jnp.exp(tot - m)


def _dir_slow(q, k, v, b, tot, st, reverse, row_ref):
    c = q.shape[0]
    row_ref[0] = b
    row_ref[1] = q
    jj = lax.broadcasted_iota(jnp.int32, (c, 1), 0)
    lane = lax.broadcasted_iota(jnp.int32, (c, c), 1)

    def body(i, s_t):
        bi = row_ref[0, pl.ds(i, 1), :]
        qi = row_ref[1, pl.ds(i, 1), :]
        valid = (jj >= i) if reverse else (jj <= i)
        w = jnp.where(valid, jnp.exp(jnp.minimum(bi - b, 0.0)), 0.0)
        col = jnp.sum(qi * k * w, axis=-1, keepdims=True)
        return jnp.where(lane == i, col, s_t)

    s_t = lax.fori_loop(0, c, body, jnp.zeros((c, c), F32))
    vb = v.astype(BF16)
    o = lax.dot_general(s_t.astype(BF16), vb, _TN, preferred_element_type=F32)
    qin = (q * jnp.exp(b)).astype(BF16)
    o = o + lax.dot_general(qin, st.astype(BF16), _NT, preferred_element_type=F32)
    kout = (k * jnp.exp(tot - b)).astype(BF16)
    u = lax.dot_general(vb, kout, _TN, preferred_element_type=F32)
    return o, st * jnp.exp(tot) + u


def _scan_kernel(qf_ref, kf_ref, vf_ref, gf_ref, qb_ref, kb_ref, vb_ref, gb_ref,
                 of_ref, ob_ref, sf_ref, sb_ref, row_ref, *, heads, dk, dv):
    c = qf_ref.shape[0]

    @pl.when(pl.program_id(1) == 0)
    def _():
        sf_ref[...] = jnp.zeros_like(sf_ref)
        sb_ref[...] = jnp.zeros_like(sb_ref)

    row = lax.broadcasted_iota(jnp.int32, (c, c), 0)
    col = lax.broadcasted_iota(jnp.int32, (c, c), 1)
    lower = row >= col
    upper = row <= col
    bf_all = _cumsum_rows(jnp.where(lower, 1.0, 0.0).astype(BF16), gf_ref[...])
    bb_all = _cumsum_rows(jnp.where(upper, 1.0, 0.0).astype(BF16), gb_ref[...])
    tot_f = bf_all[c - 1:c, :]
    tot_b = bb_all[0:1, :]
    span = jnp.maximum(jnp.max(-tot_f), jnp.max(-tot_b))
    fast = span < 2.0 * MAX_HALF_RANGE

    def run(step_f, step_b):
        for h in range(heads):
            ks = slice(h * dk, (h + 1) * dk)
            vs = slice(h * dv, (h + 1) * dv)
            o, s_new = step_f(qf_ref[:, ks], kf_ref[:, ks], vf_ref[:, vs], bf_all[:, ks], tot_f[:, ks], sf_ref[h])
            of_ref[:, vs] = o
            sf_ref[h] = s_new
            o, s_new = step_b(qb_ref[:, ks], kb_ref[:, ks], vb_ref[:, vs], bb_all[:, ks], tot_b[:, ks], sb_ref[h])
            ob_ref[:, vs] = o
            sb_ref[h] = s_new

    @pl.when(fast)
    def _():
        run(functools.partial(_dir_fast, mask=lower), functools.partial(_dir_fast, mask=upper))

    @pl.when(jnp.logical_not(fast))
    def _():
        run(functools.partial(_dir_slow, reverse=False, row_ref=row_ref),
            functools.partial(_dir_slow, reverse=True, row_ref=row_ref))


def _scan(q, kf, kb, v, gf, gb, *, batch, seq, heads, dk, dv):
    n = batch * seq
    c = min(CHUNK, seq)
    nc = seq // c
    kw, vw = heads * dk, heads * dv
    fwd = lambda blk: (lambda b, i: (b * nc + i, blk))
    bwd = lambda blk: (lambda b, i: (b * nc + (nc - 1 - i), blk))
    out_f = lambda b, i: (b * nc + i, 0)
    out_b = lambda b, i: (b * nc + (nc - 1 - i), 0)
    return pl.pallas_call(
        functools.partial(_scan_kernel, heads=heads, dk=dk, dv=dv),
        grid=(batch, nc),
        in_specs=[
            pl.BlockSpec((c, kw), fwd(q[1])), pl.BlockSpec((c, kw), fwd(kf[1])),
            pl.BlockSpec((c, vw), fwd(v[1])), pl.BlockSpec((c, kw), fwd(gf[1])),
            pl.BlockSpec((c, kw), bwd(q[1])), pl.BlockSpec((c, kw), bwd(kb[1])),
            pl.BlockSpec((c, vw), bwd(v[1])), pl.BlockSpec((c, kw), bwd(gb[1])),
        ],
        out_specs=[pl.BlockSpec((c, vw), out_f), pl.BlockSpec((c, vw), out_b)],
        out_shape=[jax.ShapeDtypeStruct((n, vw), F32), jax.ShapeDtypeStruct((n, vw), F32)],
        scratch_shapes=[
            pltpu.VMEM((heads, dv, dk), F32),
            pltpu.VMEM((heads, dv, dk), F32),
            pltpu.VMEM((2, c, dk), F32),
        ],
        compiler_params=pltpu.CompilerParams(
            dimension_semantics=("arbitrary", "arbitrary"), vmem_limit_bytes=VMEM_LIMIT),
        name=f"scan_h{heads}",
    )(q[0], kf[0], v[0], gf[0], q[0], kb[0], v[0], gb[0])


def _outproj_kernel(ofg_ref, obg_ref, ofh_ref, obh_ref, gog_ref, hog_ref, h_ref, w_ref,
                    gng_ref, hng_ref, lng_ref, lnb_ref, o_ref, y_ref):
    def head_group(of_ref, ob_ref, gate_ref, g_ref, heads, dv, base):
        for hh in range(heads):
            sl = slice(hh * dv, (hh + 1) * dv)
            o = of_ref[:, sl] + ob_ref[:, sl]
            ms = jnp.mean(o * o, axis=-1, keepdims=True)
            y = o * lax.rsqrt(ms + RMS_EPS) * g_ref[...]
            gate = gate_ref[:, sl]
            y = y * (gate * (1.0 / (1.0 + jnp.exp(-gate))))
            y_ref[:, base + hh * dv: base + (hh + 1) * dv] = y.astype(BF16)

    head_group(ofg_ref, obg_ref, gog_ref, gng_ref, GLA_HEADS, GLA_DV, 0)
    head_group(ofh_ref, obh_ref, hog_ref, hng_ref, HG_HEADS, HG_DIM, GLA_WIDTH)
    m = jnp.dot(y_ref[...], w_ref[...], preferred_element_type=F32)
    o_ref[...] = _layernorm(ALPHA * h_ref[...] + m, lng_ref[...], lnb_ref[...])


def _outproj(ofg, obg, ofh, obh, z, h, w_out, gng, hng, lng, lnb):
    n = h.shape[0]
    tm = min(OUT_TM, n)
    half = lambda i: (i, 0)
    const = lambda i: (0, 0)
    return pl.pallas_call(
        _outproj_kernel,
        grid=(n // tm,),
        in_specs=[
            pl.BlockSpec((tm, GLA_WIDTH), half), pl.BlockSpec((tm, GLA_WIDTH), half),
            pl.BlockSpec((tm, HG_WIDTH), half), pl.BlockSpec((tm, HG_WIDTH), half),
            pl.BlockSpec((tm, GLA_WIDTH), lambda i: (i, 2048 // GLA_WIDTH)),
            pl.BlockSpec((tm, HG_WIDTH), lambda i: (i, 7168 // HG_WIDTH)),
            pl.BlockSpec((tm, D_MODEL), half),
            pl.BlockSpec((D_MODEL, D_MODEL), const, pipeline_mode=pl.Buffered(1)),
            pl.BlockSpec((1, GLA_DV), const), pl.BlockSpec((1, HG_DIM), const),
            pl.BlockSpec((1, D_MODEL), const), pl.BlockSpec((1, D_MODEL), const),
        ],
        out_specs=pl.BlockSpec((tm, D_MODEL), half),
        out_shape=jax.ShapeDtypeStruct((n, D_MODEL), F32),
        scratch_shapes=[pltpu.VMEM((tm, D_MODEL), BF16)],
        compiler_params=pltpu.CompilerParams(
            dimension_semantics=("arbitrary",), vmem_limit_bytes=VMEM_LIMIT),
        name="outproj_ln1",
    )(ofg, obg, ofh, obh, z, z, h, w_out, gng, hng, lng, lnb)


def _mlp_kernel(h_ref, wup_ref, wdn_ref, lng_ref, lnb_ref, o_ref, xb_ref):
    j = pl.program_id(1)

    @pl.when(j == 0)
    def _():
        xb_ref[...] = h_ref[...].astype(BF16)

    a = jnp.dot(xb_ref[...], wup_ref[...], preferred_element_type=F32)
    a = jnp.square(jnp.maximum(a, 0.0)).astype(BF16)
    part = jnp.dot(a, wdn_ref[...], preferred_element_type=F32)

    @pl.when(j == 0)
    def _():
        o_ref[...] = part

    @pl.when(j > 0)
    def _():
        o_ref[...] += part

    @pl.when(j == pl.num_programs(1) - 1)
    def _():
        o_ref[...] = _layernorm(ALPHA * h_ref[...] + o_ref[...], lng_ref[...], lnb_ref[...])


def _mlp(h, w_up, w_down, lng, lnb):
    n = h.shape[0]
    tm = min(MLP_TM, n)
    return pl.pallas_call(
        _mlp_kernel,
        grid=(n // tm, D_FF // MLP_TF),
        in_specs=[
            pl.BlockSpec((tm, D_MODEL), lambda i, j: (i, 0)),
            pl.BlockSpec((D_MODEL, MLP_TF), lambda i, j: (0, j)),
            pl.BlockSpec((MLP_TF, D_MODEL), lambda i, j: (j, 0)),
            pl.BlockSpec((1, D_MODEL), lambda i, j: (0, 0)),
            pl.BlockSpec((1, D_MODEL), lambda i, j: (0, 0)),
        ],
        out_specs=pl.BlockSpec((tm, D_MODEL), lambda i, j: (i, 0)),
        out_shape=jax.ShapeDtypeStruct((n, D_MODEL), F32),
        scratch_shapes=[pltpu.VMEM((tm, D_MODEL), BF16)],
        compiler_params=pltpu.CompilerParams(
            dimension_semantics=("arbitrary", "arbitrary"), vmem_limit_bytes=VMEM_LIMIT),
        name="mlp_ln2",
    )(h, w_up, w_down, lng, lnb)


def _lower_bounds(p):
    c = jnp.cumsum(jax.nn.softmax(p.astype(F32), axis=0), axis=0)
    return c - c[0:1]


def _prep_layer(l, w_in, gla_w_lr2, gla_b_lr, gla_norm_g, hg_norm_g, lbs_f, lbs_b,
                w_out, ln1_g, ln1_b, w_up, w_down, ln2_g, ln2_b):
    w = w_in[l]
    glr0 = 2 * GLA_KWIDTH + 2 * GLA_WIDTH
    w_main = jnp.concatenate([w[:, :glr0], w[:, glr0 + 2 * GLA_RANK:]], axis=1).astype(BF16)
    w_glr = jnp.pad(w[:, glr0:glr0 + 2 * GLA_RANK], ((0, 0), (0, GLR_PAD - 2 * GLA_RANK))).astype(BF16)
    w2 = gla_w_lr2[l].astype(F32)
    w2c = jnp.zeros((GLR_PAD, 2 * GLA_KWIDTH), F32)
    w2c = w2c.at[:GLA_RANK, :GLA_KWIDTH].set(w2[0]).at[GLA_RANK:2 * GLA_RANK, GLA_KWIDTH:].set(w2[1])
    b2c = gla_b_lr[l].astype(F32).reshape(1, 2 * GLA_KWIDTH)
    lb = jnp.concatenate([lbs_f[l], lbs_b[l]])[None, :]
    gate_c = jnp.concatenate([jnp.log(lb), jnp.log1p(-lb), 1.0 - lb, jnp.zeros((5, 2 * HG_WIDTH), F32)], axis=0)
    row = lambda a: a[l].astype(F32)[None, :]
    return dict(
        w_main=w_main, w_glr=w_glr, w2c=w2c.astype(BF16), b2c=b2c, gate_c=gate_c,
        w_out=w_out[l].astype(BF16), gng=row(gla_norm_g), hng=row(hg_norm_g),
        ln1_g=row(ln1_g), ln1_b=row(ln1_b),
        w_up=w_up[l].astype(BF16), w_down=w_down[l].astype(BF16), ln2_g=row(ln2_g), ln2_b=row(ln2_b),
    )


def _trunk(x, layers):
    batch, seq, _ = x.shape
    h = x.reshape(batch * seq, D_MODEL).astype(F32)
    for p in layers:
        z, kfb, ga = _inproj(h, p["w_main"], p["w_glr"], p["w2c"], p["b2c"], p["gate_c"])
        ofg, obg = _scan((z, 0), (z, 1), (z, 1), (z, 1), (ga, 0), (ga, 1),
                         batch=batch, seq=seq, heads=GLA_HEADS, dk=GLA_DK, dv=GLA_DV)
        ofh, obh = _scan((z, 3), (kfb, 0), (kfb, 1), (z, 6), (z, 4), (z, 5),
                         batch=batch, seq=seq, heads=HG_HEADS, dk=HG_DIM, dv=HG_DIM)
        h = _outproj(ofg, obg, ofh, obh, z, h, p["w_out"], p["gng"], p["hng"], p["ln1_g"], p["ln1_b"])
        h = _mlp(h, p["w_up"], p["w_down"], p["ln2_g"], p["ln2_b"])
    return h.reshape(batch, seq, D_MODEL).astype(x.dtype)


def kernel(x_prompt, x_sample, w_in, gla_w_lr2, gla_b_lr, gla_norm_g, hg_norm_g, lower_bounds,
           w_out, ln1_g, ln1_b, w_up, w_down, ln2_g, ln2_b):
    lbs_f = _lower_bounds(lower_bounds[0])
    lbs_b = _lower_bounds(lower_bounds[1])
    layers = [_prep_layer(l, w_in, gla_w_lr2, gla_b_lr, gla_norm_g, hg_norm_g, lbs_f, lbs_b,
                          w_out, ln1_g, ln1_b, w_up, w_down, ln2_g, ln2_b) for l in range(DEPTH)]
    return (_trunk(x_prompt, layers), _trunk(x_sample, layers))
```

```python
import functools

import jax
import jax.numpy as jnp
from jax import lax
from jax.experimental import pallas as pl
from jax.experimental.pallas import tpu as pltpu

F32 = jnp.float32
BF16 = jnp.bfloat16

D_MODEL = 2048
DEPTH = 2
GLA_HEADS = 4
GLA_WIDTH = D_MODEL // 2
GLA_DV = GLA_WIDTH // GLA_HEADS
GLA_DK = GLA_DV // 2
GLA_KWIDTH = GLA_HEADS * GLA_DK
GLA_RANK = 16
GLA_TAU = 16.0
HG_DIM = 128
HG_WIDTH = D_MODEL - GLA_WIDTH
HG_HEADS = HG_WIDTH // HG_DIM
D_FF = 4 * D_MODEL
LN_EPS = 1e-5
RMS_EPS = 1e-6
ALPHA = (2.0 * DEPTH) ** 0.25

LANES = 128
GLR_PAD = LANES
Z_WIDTH = 8192
VMEM_LIMIT = 48 * 1024 * 1024

IN_TN = 512
IN_TM = 1024
IN_ROW_CHUNKS = 8
GATE_TILE_LO = 4096 // IN_TN
GATE_TILE_HI = 6144 // IN_TN

CHUNK = 128
MAX_HALF_RANGE = 80.0

OUT_TM = 256
MLP_TM = 512
MLP_TF = 512

_NT = (((1,), (1,)), ((), ()))
_TN = (((0,), (0,)), ((), ()))


def _log_sigmoid(x):
    return jnp.minimum(x, 0.0) - jnp.log(1.0 + jnp.exp(-jnp.abs(x)))


def _layernorm(r, g, b):
    mu = jnp.mean(r, axis=-1, keepdims=True)
    d = r - mu
    var = jnp.mean(d * d, axis=-1, keepdims=True)
    return d * lax.rsqrt(var + LN_EPS) * g + b


def _inproj_kernel(x_ref, w_ref, wglr_ref, w2_ref, b2_ref, gc_ref, z_ref, kfb_ref, ga_ref, xb_ref):
    j = pl.program_id(1)
    is_gate = jnp.logical_and(j >= GATE_TILE_LO, j < GATE_TILE_HI)
    tm = x_ref.shape[0]
    rc = tm // IN_ROW_CHUNKS
    chunks = [slice(r * rc, (r + 1) * rc) for r in range(IN_ROW_CHUNKS)]

    @pl.when(j == 0)
    def _():
        for rows in chunks:
            xb = x_ref[rows, :].astype(BF16)
            xb_ref[rows, :] = xb
            glr = jnp.dot(xb, wglr_ref[...], preferred_element_type=F32)
            a = jnp.dot(glr.astype(BF16), w2_ref[...], preferred_element_type=F32) + b2_ref[...]
            ga_ref[rows, :] = _log_sigmoid(a) * (1.0 / GLA_TAU)
            z_ref[rows, :] = jnp.dot(xb, w_ref[...], preferred_element_type=F32) * (GLA_DK ** -0.5)

    @pl.when(jnp.logical_and(j > 0, jnp.logical_not(is_gate)))
    def _():
        z_ref[...] = jnp.dot(xb_ref[...], w_ref[...], preferred_element_type=F32)

    @pl.when(is_gate)
    def _():
        log_lb = gc_ref[0:1, :]
        for rows in chunks:
            acc = jnp.dot(xb_ref[rows, :], w_ref[...], preferred_element_type=F32)
            l1p = jnp.log(1.0 + jnp.exp(-jnp.abs(acc)))
            x2 = gc_ref[1:2, :] + (jnp.minimum(acc, 0.0) - l1p)
            z_ref[rows, :] = (jnp.maximum(log_lb, x2)
                              + jnp.log(1.0 + jnp.exp(-jnp.abs(log_lb - x2))))
            kfb_ref[rows, :] = gc_ref[2:3, :] * jnp.exp(-(jnp.maximum(acc, 0.0) + l1p))


def _inproj(x, w_main, w_glr, w2c, b2c, gate_c):
    n = x.shape[0]
    tm = min(IN_TM, n)
    n_gate = GATE_TILE_HI - GATE_TILE_LO
    gate_idx = lambda i, j: jnp.clip(j - GATE_TILE_LO, 0, n_gate - 1)
    return pl.pallas_call(
        _inproj_kernel,
        grid=(n // tm, Z_WIDTH // IN_TN),
        in_specs=[
            pl.BlockSpec((tm, D_MODEL), lambda i, j: (i, 0)),
            pl.BlockSpec((D_MODEL, IN_TN), lambda i, j: (0, j)),
            pl.BlockSpec((D_MODEL, GLR_PAD), lambda i, j: (0, 0)),
            pl.BlockSpec((GLR_PAD, 2 * GLA_KWIDTH), lambda i, j: (0, 0)),
            pl.BlockSpec((1, 2 * GLA_KWIDTH), lambda i, j: (0, 0)),
            pl.BlockSpec((8, IN_TN), lambda i, j: (0, gate_idx(i, j))),
        ],
        out_specs=[
            pl.BlockSpec((tm, IN_TN), lambda i, j: (i, j)),
            pl.BlockSpec((tm, IN_TN), lambda i, j: (i, gate_idx(i, j))),
            pl.BlockSpec((tm, 2 * GLA_KWIDTH), lambda i, j: (i, 0)),
        ],
        out_shape=[
            jax.ShapeDtypeStruct((n, Z_WIDTH), F32),
            jax.ShapeDtypeStruct((n, 2 * HG_WIDTH), F32),
            jax.ShapeDtypeStruct((n, 2 * GLA_KWIDTH), F32),
        ],
        scratch_shapes=[pltpu.VMEM((tm, D_MODEL), BF16)],
        compiler_params=pltpu.CompilerParams(
            dimension_semantics=("arbitrary", "arbitrary"), vmem_limit_bytes=VMEM_LIMIT),
        name="inproj",
    )(x, w_main, w_glr, w2c, b2c, gate_c)


def _cumsum_rows(tri, g):
    hi = g.astype(BF16)
    lo = (g - hi.astype(F32)).astype(BF16)
    return (jnp.dot(tri, hi, preferred_element_type=F32) + jnp.dot(tri, lo, preferred_element_type=F32))


def _dir_fast(q, k, v, b, tot, st, mask):
    m = 0.5 * tot
    qt = (q * jnp.exp(b - m)).astype(BF16)
    kt = (k * jnp.exp(m - b)).astype(BF16)
    vb = v.astype(BF16)
    s = lax.dot_general(qt, kt, _NT, preferred_element_type=F32)
    s = jnp.where(mask, s, 0.0).astype(BF16)
    o = jnp.dot(s, vb, preferred_element_type=F32)
    ssc = (st * jnp.exp(m)).astype(BF16)
    o = o + lax.dot_general(qt, ssc, _NT, preferred_element_type=F32)
    u = lax.dot_general(vb, kt, _TN, preferred_element_type=F32)
    return o, st * jnp.exp(tot) + u * jnp.exp(tot - m)


def _dir_slow(q, k, v, b, tot, st, reverse, row_ref):
    c = q.shape[0]
    row_ref[0] = b
    row_ref[1] = q
    jj = lax.broadcasted_iota(jnp.int32, (c, 1), 0)
    lane = lax.broadcasted_iota(jnp.int32, (c, c), 1)

    def body(i, s_t):
        bi = row_ref[0, pl.ds(i, 1), :]
        qi = row_ref[1, pl.ds(i, 1), :]
        valid = (jj >= i) if reverse else (jj <= i)
        w = jnp.where(valid, jnp.exp(jnp.minimum(bi - b, 0.0)), 0.0)
        col = jnp.sum(qi * k * w, axis=-1, keepdims=True)
        return jnp.where(lane == i, col, s_t)

    s_t = lax.fori_loop(0, c, body, jnp.zeros((c, c), F32))
    vb = v.astype(BF16)
    o = lax.dot_general(s_t.astype(BF16), vb, _TN, preferred_element_type=F32)
    qin = (q * jnp.exp(b)).astype(BF16)
    o = o + lax.dot_general(qin, st.astype(BF16), _NT, preferred_element_type=F32)
    kout = (k * jnp.exp(tot - b)).astype(BF16)
    u = lax.dot_general(vb, kout, _TN, preferred_element_type=F32)
    return o, st * jnp.exp(tot) + u


def _scan_kernel(qf_ref, kf_ref, vf_ref, gf_ref, qb_ref, kb_ref, vb_ref, gb_ref,
                 of_ref, ob_ref, sf_ref, sb_ref, row_ref, *, heads, dk, dv):
    c = qf_ref.shape[0]

    @pl.when(pl.program_id(1) == 0)
    def _():
        sf_ref[...] = jnp.zeros_like(sf_ref)
        sb_ref[...] = jnp.zeros_like(sb_ref)

    row = lax.broadcasted_iota(jnp.int32, (c, c), 0)
    col = lax.broadcasted_iota(jnp.int32, (c, c), 1)
    lower = row >= col
    upper = row <= col
    bf_all = _cumsum_rows(jnp.where(lower, 1.0, 0.0).astype(BF16), gf_ref[...])
    bb_all = _cumsum_rows(jnp.where(upper, 1.0, 0.0).astype(BF16), gb_ref[...])
    tot_f = bf_all[c - 1:c, :]
    tot_b = bb_all[0:1, :]
    span = jnp.maximum(jnp.max(-tot_f), jnp.max(-tot_b))
    fast = span < 2.0 * MAX_HALF_RANGE

    def run(step_f, step_b):
        for h in range(heads):
            ks = slice(h * dk, (h + 1) * dk)
            vs = slice(h * dv, (h + 1) * dv)
            o, s_new = step_f(qf_ref[:, ks], kf_ref[:, ks], vf_ref[:, vs], bf_all[:, ks], tot_f[:, ks], sf_ref[h])
            of_ref[:, vs] = o
            sf_ref[h] = s_new
            o, s_new = step_b(qb_ref[:, ks], kb_ref[:, ks], vb_ref[:, vs], bb_all[:, ks], tot_b[:, ks], sb_ref[h])
            ob_ref[:, vs] = o
            sb_ref[h] = s_new

    @pl.when(fast)
    def _():
        run(functools.partial(_dir_fast, mask=lower), functools.partial(_dir_fast, mask=upper))

    @pl.when(jnp.logical_not(fast))
    def _():
        run(functools.partial(_dir_slow, reverse=False, row_ref=row_ref),
            functools.partial(_dir_slow, reverse=True, row_ref=row_ref))


def _scan(q, kf, kb, v, gf, gb, *, batch, seq, heads, dk, dv):
    n = batch * seq
    c = min(CHUNK, seq)
    nc = seq // c
    kw, vw = heads * dk, heads * dv
    fwd = lambda blk: (lambda b, i: (b * nc + i, blk))
    bwd = lambda blk: (lambda b, i: (b * nc + (nc - 1 - i), blk))
    out_f = lambda b, i: (b * nc + i, 0)
    out_b = lambda b, i: (b * nc + (nc - 1 - i), 0)
    return pl.pallas_call(
        functools.partial(_scan_kernel, heads=heads, dk=dk, dv=dv),
        grid=(batch, nc),
        in_specs=[
            pl.BlockSpec((c, kw), fwd(q[1])), pl.BlockSpec((c, kw), fwd(kf[1])),
            pl.BlockSpec((c, vw), fwd(v[1])), pl.BlockSpec((c, kw), fwd(gf[1])),
            pl.BlockSpec((c, kw), bwd(q[1])), pl.BlockSpec((c, kw), bwd(kb[1])),
            pl.BlockSpec((c, vw), bwd(v[1])), pl.BlockSpec((c, kw), bwd(gb[1])),
        ],
        out_specs=[pl.BlockSpec((c, vw), out_f), pl.BlockSpec((c, vw), out_b)],
        out_shape=[jax.ShapeDtypeStruct((n, vw), F32), jax.ShapeDtypeStruct((n, vw), F32)],
        scratch_shapes=[
            pltpu.VMEM((heads, dv, dk), F32),
            pltpu.VMEM((heads, dv, dk), F32),
            pltpu.VMEM((2, c, dk), F32),
        ],
        compiler_params=pltpu.CompilerParams(
            dimension_semantics=("arbitrary", "arbitrary"), vmem_limit_bytes=VMEM_LIMIT),
        name=f"scan_h{heads}",
    )(q[0], kf[0], v[0], gf[0], q[0], kb[0], v[0], gb[0])


def _outproj_kernel(ofg_ref, obg_ref, ofh_ref, obh_ref, gog_ref, hog_ref, h_ref, w_ref,
                    gng_ref, hng_ref, lng_ref, lnb_ref, o_ref, y_ref):
    def head_group(of_ref, ob_ref, gate_ref, g_ref, heads, dv, base):
        for hh in range(heads):
            sl = slice(hh * dv, (hh + 1) * dv)
            o = of_ref[:, sl] + ob_ref[:, sl]
            ms = jnp.mean(o * o, axis=-1, keepdims=True)
            y = o * lax.rsqrt(ms + RMS_EPS) * g_ref[...]
            gate = gate_ref[:, sl]
            y = y * (gate * (1.0 / (1.0 + jnp.exp(-gate))))
            y_ref[:, base + hh * dv: base + (hh + 1) * dv] = y.astype(BF16)

    head_group(ofg_ref, obg_ref, gog_ref, gng_ref, GLA_HEADS, GLA_DV, 0)
    head_group(ofh_ref, obh_ref, hog_ref, hng_ref, HG_HEADS, HG_DIM, GLA_WIDTH)
    m = jnp.dot(y_ref[...], w_ref[...], preferred_element_type=F32)
    o_ref[...] = _layernorm(ALPHA * h_ref[...] + m, lng_ref[...], lnb_ref[...])


def _outproj(ofg, obg, ofh, obh, z, h, w_out, gng, hng, lng, lnb):
    n = h.shape[0]
    tm = min(OUT_TM, n)
    half = lambda i: (i, 0)
    const = lambda i: (0, 0)
    return pl.pallas_call(
        _outproj_kernel,
        grid=(n // tm,),
        in_specs=[
            pl.BlockSpec((tm, GLA_WIDTH), half), pl.BlockSpec((tm, GLA_WIDTH), half),
            pl.BlockSpec((tm, HG_WIDTH), half), pl.BlockSpec((tm, HG_WIDTH), half),
            pl.BlockSpec((tm, GLA_WIDTH), lambda i: (i, 2048 // GLA_WIDTH)),
            pl.BlockSpec((tm, HG_WIDTH), lambda i: (i, 7168 // HG_WIDTH)),
            pl.BlockSpec((tm, D_MODEL), half),
            pl.BlockSpec((D_MODEL, D_MODEL), const, pipeline_mode=pl.Buffered(1)),
            pl.BlockSpec((1, GLA_DV), const), pl.BlockSpec((1, HG_DIM), const),
            pl.BlockSpec((1, D_MODEL), const), pl.BlockSpec((1, D_MODEL), const),
        ],
        out_specs=pl.BlockSpec((tm, D_MODEL), half),
        out_shape=jax.ShapeDtypeStruct((n, D_MODEL), F32),
        scratch_shapes=[pltpu.VMEM((tm, D_MODEL), BF16)],
        compiler_params=pltpu.CompilerParams(
            dimension_semantics=("arbitrary",), vmem_limit_bytes=VMEM_LIMIT),
        name="outproj_ln1",
    )(ofg, obg, ofh, obh, z, z, h, w_out, gng, hng, lng, lnb)


def _mlp_kernel(h_ref, wup_ref, wdn_ref, lng_ref, lnb_ref, o_ref, xb_ref):
    j = pl.program_id(1)

    @pl.when(j == 0)
    def _():
        xb_ref[...] = h_ref[...].astype(BF16)
        o_ref[...] = jnp.zeros_like(o_ref)

    a = jnp.dot(xb_ref[...], wup_ref[...], preferred_element_type=F32)
    a = jnp.square(jnp.maximum(a, 0.0)).astype(BF16)
    o_ref[...] += jnp.dot(a, wdn_ref[...], preferred_element_type=F32)

    @pl.when(j == pl.num_programs(1) - 1)
    def _():
        o_ref[...] = _layernorm(ALPHA * h_ref[...] + o_ref[...], lng_ref[...], lnb_ref[...])


def _mlp(h, w_up, w_down, lng, lnb):
    n = h.shape[0]
    tm = min(MLP_TM, n)
    return pl.pallas_call(
        _mlp_kernel,
        grid=(n // tm, D_FF // MLP_TF),
        in_specs=[
            pl.BlockSpec((tm, D_MODEL), lambda i, j: (i, 0)),
            pl.BlockSpec((D_MODEL, MLP_TF), lambda i, j: (0, j)),
            pl.BlockSpec((MLP_TF, D_MODEL), lambda i, j: (j, 0)),
            pl.BlockSpec((1, D_MODEL), lambda i, j: (0, 0)),
            pl.BlockSpec((1, D_MODEL), lambda i, j: (0, 0)),
        ],
        out_specs=pl.BlockSpec((tm, D_MODEL), lambda i, j: (i, 0)),
        out_shape=jax.ShapeDtypeStruct((n, D_MODEL), F32),
        scratch_shapes=[pltpu.VMEM((tm, D_MODEL), BF16)],
        compiler_params=pltpu.CompilerParams(
            dimension_semantics=("arbitrary", "arbitrary"), vmem_limit_bytes=VMEM_LIMIT),
        name="mlp_ln2",
    )(h, w_up, w_down, lng, lnb)


def _lower_bounds(p):
    c = jnp.cumsum(jax.nn.softmax(p.astype(F32), axis=0), axis=0)
    return c - c[0:1]


def _prep_layer(l, w_in, gla_w_lr2, gla_b_lr, gla_norm_g, hg_norm_g, lbs_f, lbs_b,
                w_out, ln1_g, ln1_b, w_up, w_down, ln2_g, ln2_b):
    w = w_in[l]
    glr0 = 2 * GLA_KWIDTH + 2 * GLA_WIDTH
    w_main = jnp.concatenate([w[:, :glr0], w[:, glr0 + 2 * GLA_RANK:]], axis=1).astype(BF16)
    w_glr = jnp.pad(w[:, glr0:glr0 + 2 * GLA_RANK], ((0, 0), (0, GLR_PAD - 2 * GLA_RANK))).astype(BF16)
    w2 = gla_w_lr2[l].astype(F32)
    w2c = jnp.zeros((GLR_PAD, 2 * GLA_KWIDTH), F32)
    w2c = w2c.at[:GLA_RANK, :GLA_KWIDTH].set(w2[0]).at[GLA_RANK:2 * GLA_RANK, GLA_KWIDTH:].set(w2[1])
    b2c = gla_b_lr[l].astype(F32).reshape(1, 2 * GLA_KWIDTH)
    lb = jnp.concatenate([lbs_f[l], lbs_b[l]])[None, :]
    gate_c = jnp.concatenate([jnp.log(lb), jnp.log1p(-lb), 1.0 - lb, jnp.zeros((5, 2 * HG_WIDTH), F32)], axis=0)
    row = lambda a: a[l].astype(F32)[None, :]
    return dict(
        w_main=w_main, w_glr=w_glr, w2c=w2c.astype(BF16), b2c=b2c, gate_c=gate_c,
        w_out=w_out[l].astype(BF16), gng=row(gla_norm_g), hng=row(hg_norm_g),
        ln1_g=row(ln1_g), ln1_b=row(ln1_b),
        w_up=w_up[l].astype(BF16), w_down=w_down[l].astype(BF16), ln2_g=row(ln2_g), ln2_b=row(ln2_b),
    )


def _trunk(x, layers):
    batch, seq, _ = x.shape
    h = x.reshape(batch * seq, D_MODEL).astype(F32)
    for p in layers:
        z, kfb, ga = _inproj(h, p["w_main"], p["w_glr"], p["w2c"], p["b2c"], p["gate_c"])
        ofg, obg = _scan((z, 0), (z, 1), (z, 1), (z, 1), (ga, 0), (ga, 1),
                         batch=batch, seq=seq, heads=GLA_HEADS, dk=GLA_DK, dv=GLA_DV)
        ofh, obh = _scan((z, 3), (kfb, 0), (kfb, 1), (z, 6), (z, 4), (z, 5),
                         batch=batch, seq=seq, heads=HG_HEADS, dk=HG_DIM, dv=HG_DIM)
        h = _outproj(ofg, obg, ofh, obh, z, h, p["w_out"], p["gng"], p["hng"], p["ln1_g"], p["ln1_b"])
        h = _mlp(h, p["w_up"], p["w_down"], p["ln2_g"], p["ln2_b"])
    return h.reshape(batch, seq, D_MODEL).astype(x.dtype)


def kernel(x_prompt, x_sample, w_in, gla_w_lr2, gla_b_lr, gla_norm_g, hg_norm_g, lower_bounds,
           w_out, ln1_g, ln1_b, w_up, w_down, ln2_g, ln2_b):
    lbs_f = _lower_bounds(lower_bounds[0])
    lbs_b = _lower_bounds(lower_bounds[1])
    layers = [_prep_layer(l, w_in, gla_w_lr2, gla_b_lr, gla_norm_g, hg_norm_g, lbs_f, lbs_b,
                          w_out, ln1_g, ln1_b, w_up, w_down, ln2_g, ln2_b) for l in range(DEPTH)]
    return (_trunk(x_prompt, layers), _trunk(x_sample, layers))
```

```python
import functools

import jax
import jax.numpy as jnp
from jax import lax
from jax.experimental import pallas as pl
from jax.experimental.pallas import tpu as pltpu

F32 = jnp.float32
BF16 = jnp.bfloat16

D_MODEL = 2048
DEPTH = 2
GLA_HEADS = 4
GLA_WIDTH = D_MODEL // 2
GLA_DV = GLA_WIDTH // GLA_HEADS
GLA_DK = GLA_DV // 2
GLA_KWIDTH = GLA_HEADS * GLA_DK
GLA_RANK = 16
GLA_TAU = 16.0
HG_DIM = 128
HG_WIDTH = D_MODEL - GLA_WIDTH
HG_HEADS = HG_WIDTH // HG_DIM
D_FF = 4 * D_MODEL
LN_EPS = 1e-5
RMS_EPS = 1e-6
ALPHA = (2.0 * DEPTH) ** 0.25

LANES = 128
GLR_PAD = LANES
Z_WIDTH = 8192
VMEM_LIMIT = 48 * 1024 * 1024

IN_TN = 512
IN_TM = 1024
IN_ROW_CHUNKS = 8
GATE_TILE_LO = 4096 // IN_TN
GATE_TILE_HI = 6144 // IN_TN

CHUNK = 128
MAX_HALF_RANGE = 80.0

OUT_TM = 512
MLP_TM = 1024
MLP_TF = 512
MLP_VMEM_LIMIT = 58 * 1024 * 1024

_NT = (((1,), (1,)), ((), ()))
_TN = (((0,), (0,)), ((), ()))


def _log_sigmoid(x):
    return jnp.minimum(x, 0.0) - jnp.log(1.0 + jnp.exp(-jnp.abs(x)))


def _layernorm(r, g, b):
    mu = jnp.mean(r, axis=-1, keepdims=True)
    d = r - mu
    var = jnp.mean(d * d, axis=-1, keepdims=True)
    return d * lax.rsqrt(var + LN_EPS) * g + b


def _inproj_kernel(x_ref, w_ref, wglr_ref, w2_ref, b2_ref, gc_ref, z_ref, logf_ref, ga_ref, xb_ref):
    j = pl.program_id(1)
    is_gate = jnp.logical_and(j >= GATE_TILE_LO, j < GATE_TILE_HI)
    tm = x_ref.shape[0]
    rc = tm // IN_ROW_CHUNKS
    chunks = [slice(r * rc, (r + 1) * rc) for r in range(IN_ROW_CHUNKS)]

    @pl.when(j == 0)
    def _():
        for rows in chunks:
            xb = x_ref[rows, :].astype(BF16)
            xb_ref[rows, :] = xb
            glr = jnp.dot(xb, wglr_ref[...], preferred_element_type=F32)
            a = jnp.dot(glr.astype(BF16), w2_ref[...], preferred_element_type=F32) + b2_ref[...]
            ga_ref[rows, :] = _log_sigmoid(a) * (1.0 / GLA_TAU)
            q = jnp.dot(xb, w_ref[...], preferred_element_type=F32) * (GLA_DK ** -0.5)
            z_ref[rows, :] = q.astype(BF16)

    @pl.when(jnp.logical_and(j > 0, jnp.logical_not(is_gate)))
    def _():
        z_ref[...] = jnp.dot(xb_ref[...], w_ref[...], preferred_element_type=F32).astype(BF16)

    @pl.when(is_gate)
    def _():
        log_lb = gc_ref[0:1, :]
        for rows in chunks:
            acc = jnp.dot(xb_ref[rows, :], w_ref[...], preferred_element_type=F32)
            l1p = jnp.log(1.0 + jnp.exp(-jnp.abs(acc)))
            x2 = gc_ref[1:2, :] + (jnp.minimum(acc, 0.0) - l1p)
            logf_ref[rows, :] = (jnp.maximum(log_lb, x2)
                                 + jnp.log(1.0 + jnp.exp(-jnp.abs(log_lb - x2))))
            z_ref[rows, :] = (gc_ref[2:3, :] * jnp.exp(-(jnp.maximum(acc, 0.0) + l1p))).astype(BF16)


def _inproj(x, w_main, w_glr, w2c, b2c, gate_c):
    n = x.shape[0]
    tm = min(IN_TM, n)
    n_gate = GATE_TILE_HI - GATE_TILE_LO
    gate_idx = lambda i, j: jnp.clip(j - GATE_TILE_LO, 0, n_gate - 1)
    return pl.pallas_call(
        _inproj_kernel,
        grid=(n // tm, Z_WIDTH // IN_TN),
        in_specs=[
            pl.BlockSpec((tm, D_MODEL), lambda i, j: (i, 0)),
            pl.BlockSpec((None, D_MODEL, IN_TN), lambda i, j: (j, 0, 0)),
            pl.BlockSpec((D_MODEL, GLR_PAD), lambda i, j: (0, 0)),
            pl.BlockSpec((GLR_PAD, 2 * GLA_KWIDTH), lambda i, j: (0, 0)),
            pl.BlockSpec((1, 2 * GLA_KWIDTH), lambda i, j: (0, 0)),
            pl.BlockSpec((8, IN_TN), lambda i, j: (0, gate_idx(i, j))),
        ],
        out_specs=[
            pl.BlockSpec((tm, IN_TN), lambda i, j: (i, j)),
            pl.BlockSpec((tm, IN_TN), lambda i, j: (i, gate_idx(i, j))),
            pl.BlockSpec((tm, 2 * GLA_KWIDTH), lambda i, j: (i, 0)),
        ],
        out_shape=[
            jax.ShapeDtypeStruct((n, Z_WIDTH), BF16),
            jax.ShapeDtypeStruct((n, 2 * HG_WIDTH), F32),
            jax.ShapeDtypeStruct((n, 2 * GLA_KWIDTH), F32),
        ],
        scratch_shapes=[pltpu.VMEM((tm, D_MODEL), BF16)],
        compiler_params=pltpu.CompilerParams(
            dimension_semantics=("arbitrary", "arbitrary"), vmem_limit_bytes=VMEM_LIMIT),
        name="inproj",
    )(x, w_main, w_glr, w2c, b2c, gate_c)


def _cumsum_rows(tri, g):
    hi = g.astype(BF16)
    lo = (g - hi.astype(F32)).astype(BF16)
    return (jnp.dot(tri, hi, preferred_element_type=F32) + jnp.dot(tri, lo, preferred_element_type=F32))


def _dir_fast(q, k, v, b, tot, st, mask):
    m = 0.5 * tot
    qt = q * jnp.exp(b - m).astype(BF16)
    kt = k * jnp.exp(m - b).astype(BF16)
    s = lax.dot_general(qt, kt, _NT, preferred_element_type=F32)
    s = jnp.where(mask, s, 0.0).astype(BF16)
    o = jnp.dot(s, v, preferred_element_type=F32)
    ssc = (st * jnp.exp(m)).astype(BF16)
    o = o + lax.dot_general(qt, ssc, _NT, preferred_element_type=F32)
    u = lax.dot_general(v, kt, _TN, preferred_element_type=F32)
    return o, st * jnp.exp(tot) + u * jnp.exp(tot - m)


def _dir_slow(q, k, v, b, tot, st, reverse, row_ref):
    c = q.shape[0]
    q = q.astype(F32)
    k = k.astype(F32)
    row_ref[0] = b
    row_ref[1] = q
    jj = lax.broadcasted_iota(jnp.int32, (c, 1), 0)
    lane = lax.broadcasted_iota(jnp.int32, (c, c), 1)

    def body(i, s_t):
        bi = row_ref[0, pl.ds(i, 1), :]
        qi = row_ref[1, pl.ds(i, 1), :]
        valid = (jj >= i) if reverse else (jj <= i)
        w = jnp.where(valid, jnp.exp(jnp.minimum(bi - b, 0.0)), 0.0)
        col = jnp.sum(qi * k * w, axis=-1, keepdims=True)
        return jnp.where(lane == i, col, s_t)

    s_t = lax.fori_loop(0, c, body, jnp.zeros((c, c), F32))
    o = lax.dot_general(s_t.astype(BF16), v, _TN, preferred_element_type=F32)
    qin = (q * jnp.exp(b)).astype(BF16)
    o = o + lax.dot_general(qin, st.astype(BF16), _NT, preferred_element_type=F32)
    kout = (k * jnp.exp(tot - b)).astype(BF16)
    u = lax.dot_general(v, kout, _TN, preferred_element_type=F32)
    return o, st * jnp.exp(tot) + u


def _scan_kernel(qf_ref, kf_ref, vf_ref, gf_ref, qb_ref, kb_ref, vb_ref, gb_ref,
                 of_ref, ob_ref, sf_ref, sb_ref, row_ref, *, heads, dk, dv):
    c = qf_ref.shape[0]

    @pl.when(pl.program_id(1) == 0)
    def _():
        sf_ref[...] = jnp.zeros_like(sf_ref)
        sb_ref[...] = jnp.zeros_like(sb_ref)

    row = lax.broadcasted_iota(jnp.int32, (c, c), 0)
    col = lax.broadcasted_iota(jnp.int32, (c, c), 1)
    lower = row >= col
    upper = row <= col
    bf_all = _cumsum_rows(jnp.where(lower, 1.0, 0.0).astype(BF16), gf_ref[...])
    bb_all = _cumsum_rows(jnp.where(upper, 1.0, 0.0).astype(BF16), gb_ref[...])
    tot_f = bf_all[c - 1:c, :]
    tot_b = bb_all[0:1, :]
    span = jnp.maximum(jnp.max(-tot_f), jnp.max(-tot_b))
    fast = span < 2.0 * MAX_HALF_RANGE

    def run(step_f, step_b):
        for h in range(heads):
            ks = slice(h * dk, (h + 1) * dk)
            vs = slice(h * dv, (h + 1) * dv)
            o, s_new = step_f(qf_ref[:, ks], kf_ref[:, ks], vf_ref[:, vs], bf_all[:, ks], tot_f[:, ks], sf_ref[h])
            of_ref[:, vs] = o.astype(BF16)
            sf_ref[h] = s_new
            o, s_new = step_b(qb_ref[:, ks], kb_ref[:, ks], vb_ref[:, vs], bb_all[:, ks], tot_b[:, ks], sb_ref[h])
            ob_ref[:, vs] = o.astype(BF16)
            sb_ref[h] = s_new

    @pl.when(fast)
    def _():
        run(functools.partial(_dir_fast, mask=lower), functools.partial(_dir_fast, mask=upper))

    @pl.when(jnp.logical_not(fast))
    def _():
        run(functools.partial(_dir_slow, reverse=False, row_ref=row_ref),
            functools.partial(_dir_slow, reverse=True, row_ref=row_ref))


def _scan(q, kf, kb, v, gf, gb, *, batch, seq, heads, dk, dv):
    n = batch * seq
    c = min(CHUNK, seq)
    nc = seq // c
    kw, vw = heads * dk, heads * dv
    fwd = lambda blk: (lambda b, i: (b * nc + i, blk))
    bwd = lambda blk: (lambda b, i: (b * nc + (nc - 1 - i), blk))
    out_f = lambda b, i: (b * nc + i, 0)
    out_b = lambda b, i: (b * nc + (nc - 1 - i), 0)
    return pl.pallas_call(
        functools.partial(_scan_kernel, heads=heads, dk=dk, dv=dv),
        grid=(batch, nc),
        in_specs=[
            pl.BlockSpec((c, kw), fwd(q[1])), pl.BlockSpec((c, kw), fwd(kf[1])),
            pl.BlockSpec((c, vw), fwd(v[1])), pl.BlockSpec((c, kw), fwd(gf[1])),
            pl.BlockSpec((c, kw), bwd(q[1])), pl.BlockSpec((c, kw), bwd(kb[1])),
            pl.BlockSpec((c, vw), bwd(v[1])), pl.BlockSpec((c, kw), bwd(gb[1])),
        ],
        out_specs=[pl.BlockSpec((c, vw), out_f), pl.BlockSpec((c, vw), out_b)],
        out_shape=[jax.ShapeDtypeStruct((n, vw), BF16), jax.ShapeDtypeStruct((n, vw), BF16)],
        scratch_shapes=[
            pltpu.VMEM((heads, dv, dk), F32),
            pltpu.VMEM((heads, dv, dk), F32),
            pltpu.VMEM((2, c, dk), F32),
        ],
        compiler_params=pltpu.CompilerParams(
            dimension_semantics=("arbitrary", "arbitrary"), vmem_limit_bytes=VMEM_LIMIT),
        name=f"scan_h{heads}",
    )(q[0], kf[0], v[0], gf[0], q[0], kb[0], v[0], gb[0])


def _outproj_kernel(ofg_ref, obg_ref, ofh_ref, obh_ref, gog_ref, hog_ref, h_ref, w_ref,
                    gng_ref, hng_ref, lng_ref, lnb_ref, o_ref, y_ref):
    def head_group(of_ref, ob_ref, gate_ref, g_ref, heads, dv, base):
        for hh in range(heads):
            sl = slice(hh * dv, (hh + 1) * dv)
            o = of_ref[:, sl].astype(F32) + ob_ref[:, sl].astype(F32)
            ms = jnp.mean(o * o, axis=-1, keepdims=True)
            y = o * lax.rsqrt(ms + RMS_EPS) * g_ref[...]
            gate = gate_ref[:, sl].astype(F32)
            y = y * (gate * (1.0 / (1.0 + jnp.exp(-gate))))
            y_ref[:, base + hh * dv: base + (hh + 1) * dv] = y.astype(BF16)

    head_group(ofg_ref, obg_ref, gog_ref, gng_ref, GLA_HEADS, GLA_DV, 0)
    head_group(ofh_ref, obh_ref, hog_ref, hng_ref, HG_HEADS, HG_DIM, GLA_WIDTH)
    m = jnp.dot(y_ref[...], w_ref[...], preferred_element_type=F32)
    o_ref[...] = _layernorm(ALPHA * h_ref[...] + m, lng_ref[...], lnb_ref[...])


def _outproj(ofg, obg, ofh, obh, z, h, w_out, gng, hng, lng, lnb):
    n = h.shape[0]
    tm = min(OUT_TM, n)
    half = lambda i: (i, 0)
    const = lambda i: (0, 0)
    return pl.pallas_call(
        _outproj_kernel,
        grid=(n // tm,),
        in_specs=[
            pl.BlockSpec((tm, GLA_WIDTH), half), pl.BlockSpec((tm, GLA_WIDTH), half),
            pl.BlockSpec((tm, HG_WIDTH), half), pl.BlockSpec((tm, HG_WIDTH), half),
            pl.BlockSpec((tm, GLA_WIDTH), lambda i: (i, 2048 // GLA_WIDTH)),
            pl.BlockSpec((tm, HG_WIDTH), lambda i: (i, 7168 // HG_WIDTH)),
            pl.BlockSpec((tm, D_MODEL), half),
            pl.BlockSpec((D_MODEL, D_MODEL), const, pipeline_mode=pl.Buffered(1)),
            pl.BlockSpec((1, GLA_DV), const), pl.BlockSpec((1, HG_DIM), const),
            pl.BlockSpec((1, D_MODEL), const), pl.BlockSpec((1, D_MODEL), const),
        ],
        out_specs=pl.BlockSpec((tm, D_MODEL), half),
        out_shape=jax.ShapeDtypeStruct((n, D_MODEL), F32),
        scratch_shapes=[pltpu.VMEM((tm, D_MODEL), BF16)],
        compiler_params=pltpu.CompilerParams(
            dimension_semantics=("arbitrary",), vmem_limit_bytes=VMEM_LIMIT),
        name="outproj_ln1",
    )(ofg, obg, ofh, obh, z, z, h, w_out, gng, hng, lng, lnb)


def _mlp_kernel(h_ref, wup_ref, wdn_ref, lng_ref, lnb_ref, o_ref, xb_ref):
    j = pl.program_id(1)

    @pl.when(j == 0)
    def _():
        xb_ref[...] = h_ref[...].astype(BF16)
        o_ref[...] = jnp.zeros_like(o_ref)

    a = jnp.dot(xb_ref[...], wup_ref[...], preferred_element_type=F32)
    a = jnp.square(jnp.maximum(a, 0.0)).astype(BF16)
    o_ref[...] += jnp.dot(a, wdn_ref[...], preferred_element_type=F32)

    @pl.when(j == pl.num_programs(1) - 1)
    def _():
        o_ref[...] = _layernorm(ALPHA * h_ref[...] + o_ref[...], lng_ref[...], lnb_ref[...])


def _mlp(h, w_up, w_down, lng, lnb):
    n = h.shape[0]
    tm = min(MLP_TM, n)
    return pl.pallas_call(
        _mlp_kernel,
        grid=(n // tm, D_FF // MLP_TF),
        in_specs=[
            pl.BlockSpec((tm, D_MODEL), lambda i, j: (i, 0)),
            pl.BlockSpec((None, D_MODEL, MLP_TF), lambda i, j: (j, 0, 0)),
            pl.BlockSpec((MLP_TF, D_MODEL), lambda i, j: (j, 0)),
            pl.BlockSpec((1, D_MODEL), lambda i, j: (0, 0)),
            pl.BlockSpec((1, D_MODEL), lambda i, j: (0, 0)),
        ],
        out_specs=pl.BlockSpec((tm, D_MODEL), lambda i, j: (i, 0)),
        out_shape=jax.ShapeDtypeStruct((n, D_MODEL), F32),
        scratch_shapes=[pltpu.VMEM((tm, D_MODEL), BF16)],
        compiler_params=pltpu.CompilerParams(
            dimension_semantics=("arbitrary", "arbitrary"), vmem_limit_bytes=MLP_VMEM_LIMIT),
        name="mlp_ln2",
    )(h, w_up, w_down, lng, lnb)


def _lower_bounds(p):
    c = jnp.cumsum(jax.nn.softmax(p.astype(F32), axis=0), axis=0)
    return c - c[0:1]


def _prep_layer(l, w_in, gla_w_lr2, gla_b_lr, gla_norm_g, hg_norm_g, lbs_f, lbs_b,
                w_out, ln1_g, ln1_b, w_up, w_down, ln2_g, ln2_b):
    w = w_in[l]
    glr0 = 2 * GLA_KWIDTH + 2 * GLA_WIDTH
    w_main = jnp.concatenate([w[:, :glr0], w[:, glr0 + 2 * GLA_RANK:]], axis=1).astype(BF16)
    tile_major = lambda a, tn: a.reshape(a.shape[0], a.shape[1] // tn, tn).transpose(1, 0, 2)
    w_glr = jnp.pad(w[:, glr0:glr0 + 2 * GLA_RANK], ((0, 0), (0, GLR_PAD - 2 * GLA_RANK))).astype(BF16)
    w2 = gla_w_lr2[l].astype(F32)
    w2c = jnp.zeros((GLR_PAD, 2 * GLA_KWIDTH), F32)
    w2c = w2c.at[:GLA_RANK, :GLA_KWIDTH].set(w2[0]).at[GLA_RANK:2 * GLA_RANK, GLA_KWIDTH:].set(w2[1])
    b2c = gla_b_lr[l].astype(F32).reshape(1, 2 * GLA_KWIDTH)
    lb = jnp.concatenate([lbs_f[l], lbs_b[l]])[None, :]
    gate_c = jnp.concatenate([jnp.log(lb), jnp.log1p(-lb), 1.0 - lb, jnp.zeros((5, 2 * HG_WIDTH), F32)], axis=0)
    row = lambda a: a[l].astype(F32)[None, :]
    return dict(
        w_main=tile_major(w_main, IN_TN), w_glr=w_glr, w2c=w2c.astype(BF16), b2c=b2c, gate_c=gate_c,
        w_out=w_out[l].astype(BF16), gng=row(gla_norm_g), hng=row(hg_norm_g),
        ln1_g=row(ln1_g), ln1_b=row(ln1_b),
        w_up=tile_major(w_up[l].astype(BF16), MLP_TF), w_down=w_down[l].astype(BF16),
        ln2_g=row(ln2_g), ln2_b=row(ln2_b),
    )


def _trunk(x, layers):
    batch, seq, _ = x.shape
    h = x.reshape(batch * seq, D_MODEL).astype(F32)
    for p in layers:
        z, logf, ga = _inproj(h, p["w_main"], p["w_glr"], p["w2c"], p["b2c"], p["gate_c"])
        ofg, obg = _scan((z, 0), (z, 1), (z, 1), (z, 1), (ga, 0), (ga, 1),
                         batch=batch, seq=seq, heads=GLA_HEADS, dk=GLA_DK, dv=GLA_DV)
        ofh, obh = _scan((z, 3), (z, 4), (z, 5), (z, 6), (logf, 0), (logf, 1),
                         batch=batch, seq=seq, heads=HG_HEADS, dk=HG_DIM, dv=HG_DIM)
        h = _outproj(ofg, obg, ofh, obh, z, h, p["w_out"], p["gng"], p["hng"], p["ln1_g"], p["ln1_b"])
        h = _mlp(h, p["w_up"], p["w_down"], p["ln2_g"], p["ln2_b"])
    return h.reshape(batch, seq, D_MODEL).astype(x.dtype)


def kernel(x_prompt, x_sample, w_in, gla_w_lr2, gla_b_lr, gla_norm_g, hg_norm_g, lower_bounds,
           w_out, ln1_g, ln1_b, w_up, w_down, ln2_g, ln2_b):
    lbs_f = _lower_bounds(lower_bounds[0])
    lbs_b = _lower_bounds(lower_bounds[1])
    layers = [_prep_layer(l, w_in, gla_w_lr2, gla_b_lr, gla_norm_g, hg_norm_g, lbs_f, lbs_b,
                          w_out, ln1_g, ln1_b, w_up, w_down, ln2_g, ln2_b) for l in range(DEPTH)]
    return (_trunk(x_prompt, layers), _trunk(x_sample, layers))
```

```python
import functools

import jax
import jax.numpy as jnp
from jax import lax
from jax.experimental import pallas as pl
from jax.experimental.pallas import tpu as pltpu

F32 = jnp.float32
BF16 = jnp.bfloat16

D_MODEL = 2048
DEPTH = 2
GLA_HEADS = 4
GLA_WIDTH = D_MODEL // 2
GLA_DV = GLA_WIDTH // GLA_HEADS
GLA_DK = GLA_DV // 2
GLA_KWIDTH = GLA_HEADS * GLA_DK
GLA_RANK = 16
GLA_TAU = 16.0
HG_DIM = 128
HG_WIDTH = D_MODEL - GLA_WIDTH
HG_HEADS = HG_WIDTH // HG_DIM
D_FF = 4 * D_MODEL
LN_EPS = 1e-5
RMS_EPS = 1e-6
F32_TINY = 1.1754943508222875e-38
ALPHA = (2.0 * DEPTH) ** 0.25

LANES = 128
GLR_PAD = LANES
Z_WIDTH = 8192
VMEM_LIMIT = 48 * 1024 * 1024

IN_TN = 512
IN_TM = 1024
IN_ROW_CHUNKS = 8
GATE_TILE_LO = 4096 // IN_TN
GATE_TILE_HI = 6144 // IN_TN

CHUNK = 128
MAX_HALF_RANGE = 80.0

OUT_TM = 512
MLP_TM = 1024
MLP_TF = 512
MLP_VMEM_LIMIT = 58 * 1024 * 1024

_NT = (((1,), (1,)), ((), ()))
_TN = (((0,), (0,)), ((), ()))


def _log_sigmoid(x):
    return jnp.minimum(x, 0.0) - jnp.log(1.0 + jnp.exp(-jnp.abs(x)))


def _layernorm(r, g, b):
    mu = jnp.mean(r, axis=-1, keepdims=True)
    d = r - mu
    var = jnp.mean(d * d, axis=-1, keepdims=True)
    return d * lax.rsqrt(var + LN_EPS) * g + b


def _inproj_kernel(x_ref, w_ref, wglr_ref, w2_ref, b2_ref, gc_ref, z_ref, logf_ref, ga_ref, xb_ref):
    j = pl.program_id(1)
    is_gate = jnp.logical_and(j >= GATE_TILE_LO, j < GATE_TILE_HI)
    tm = x_ref.shape[0]
    rc = tm // IN_ROW_CHUNKS
    chunks = [slice(r * rc, (r + 1) * rc) for r in range(IN_ROW_CHUNKS)]

    @pl.when(j == 0)
    def _():
        for rows in chunks:
            xb = x_ref[rows, :].astype(BF16)
            xb_ref[rows, :] = xb
            glr = jnp.dot(xb, wglr_ref[...], preferred_element_type=F32)
            a = jnp.dot(glr.astype(BF16), w2_ref[...], preferred_element_type=F32) + b2_ref[...]
            ga_ref[rows, :] = _log_sigmoid(a) * (1.0 / GLA_TAU)
            q = jnp.dot(xb, w_ref[...], preferred_element_type=F32) * (GLA_DK ** -0.5)
            z_ref[rows, :] = q.astype(BF16)

    @pl.when(jnp.logical_and(j > 0, jnp.logical_not(is_gate)))
    def _():
        z_ref[...] = jnp.dot(xb_ref[...], w_ref[...], preferred_element_type=F32).astype(BF16)

    @pl.when(is_gate)
    def _():
        lb = gc_ref[0:1, :]
        one_minus_lb = gc_ref[1:2, :]
        f_min = None
        for rows in chunks:
            acc = jnp.dot(xb_ref[rows, :], w_ref[...], preferred_element_type=F32)
            e = jnp.exp(-jnp.abs(acc))
            r = 1.0 / (1.0 + e)
            er = e * r
            pos = acc >= 0.0
            f = lb + one_minus_lb * jnp.where(pos, r, er)
            logf_ref[rows, :] = jnp.log(f)
            z_ref[rows, :] = (one_minus_lb * jnp.where(pos, er, r)).astype(BF16)
            f_min = f if f_min is None else jnp.minimum(f_min, f)

        @pl.when(jnp.logical_not(jnp.min(f_min) >= F32_TINY))
        def _():
            acc = jnp.dot(xb_ref[...], w_ref[...], preferred_element_type=F32)
            log_lb = gc_ref[2:3, :]
            x2 = gc_ref[3:4, :] + _log_sigmoid(acc)
            logf_ref[...] = (jnp.maximum(log_lb, x2)
                             + jnp.log(1.0 + jnp.exp(-jnp.abs(log_lb - x2))))


def _inproj(x, w_main, w_glr, w2c, b2c, gate_c):
    n = x.shape[0]
    tm = min(IN_TM, n)
    n_gate = GATE_TILE_HI - GATE_TILE_LO
    gate_idx = lambda i, j: jnp.clip(j - GATE_TILE_LO, 0, n_gate - 1)
    return pl.pallas_call(
        _inproj_kernel,
        grid=(n // tm, Z_WIDTH // IN_TN),
        in_specs=[
            pl.BlockSpec((tm, D_MODEL), lambda i, j: (i, 0)),
            pl.BlockSpec((D_MODEL, IN_TN), lambda i, j: (0, j)),
            pl.BlockSpec((D_MODEL, GLR_PAD), lambda i, j: (0, 0)),
            pl.BlockSpec((GLR_PAD, 2 * GLA_KWIDTH), lambda i, j: (0, 0)),
            pl.BlockSpec((1, 2 * GLA_KWIDTH), lambda i, j: (0, 0)),
            pl.BlockSpec((8, IN_TN), lambda i, j: (0, gate_idx(i, j))),
        ],
        out_specs=[
            pl.BlockSpec((tm, IN_TN), lambda i, j: (i, j)),
            pl.BlockSpec((tm, IN_TN), lambda i, j: (i, gate_idx(i, j))),
            pl.BlockSpec((tm, 2 * GLA_KWIDTH), lambda i, j: (i, 0)),
        ],
        out_shape=[
            jax.ShapeDtypeStruct((n, Z_WIDTH), BF16),
            jax.ShapeDtypeStruct((n, 2 * HG_WIDTH), F32),
            jax.ShapeDtypeStruct((n, 2 * GLA_KWIDTH), F32),
        ],
        scratch_shapes=[pltpu.VMEM((tm, D_MODEL), BF16)],
        compiler_params=pltpu.CompilerParams(
            dimension_semantics=("arbitrary", "arbitrary"), vmem_limit_bytes=VMEM_LIMIT),
        name="inproj",
    )(x, w_main, w_glr, w2c, b2c, gate_c)


def _cumsum_rows(tri, g):
    hi = g.astype(BF16)
    lo = (g - hi.astype(F32)).astype(BF16)
    return (jnp.dot(tri, hi, preferred_element_type=F32) + jnp.dot(tri, lo, preferred_element_type=F32))


def _dir_fast(q, k, v, b, tot, st, mask):
    m = 0.5 * tot
    qt = q * jnp.exp(b - m).astype(BF16)
    kt = k * jnp.exp(m - b).astype(BF16)
    s = lax.dot_general(qt, kt, _NT, preferred_element_type=F32)
    s = jnp.where(mask, s, 0.0).astype(BF16)
    o = jnp.dot(s, v, preferred_element_type=F32)
    ssc = (st * jnp.exp(m)).astype(BF16)
    o = o + lax.dot_general(qt, ssc, _NT, preferred_element_type=F32)
    u = lax.dot_general(v, kt, _TN, preferred_element_type=F32)
    return o, st * jnp.exp(tot) + u * jnp.exp(tot - m)


def _dir_slow(q, k, v, b, tot, st, reverse, row_ref):
    c = q.shape[0]
    q = q.astype(F32)
    k = k.astype(F32)
    row_ref[0] = b
    row_ref[1] = q
    jj = lax.broadcasted_iota(jnp.int32, (c, 1), 0)
    lane = lax.broadcasted_iota(jnp.int32, (c, c), 1)

    def body(i, s_t):
        bi = row_ref[0, pl.ds(i, 1), :]
        qi = row_ref[1, pl.ds(i, 1), :]
        valid = (jj >= i) if reverse else (jj <= i)
        w = jnp.where(valid, jnp.exp(jnp.minimum(bi - b, 0.0)), 0.0)
        col = jnp.sum(qi * k * w, axis=-1, keepdims=True)
        return jnp.where(lane == i, col, s_t)

    s_t = lax.fori_loop(0, c, body, jnp.zeros((c, c), F32))
    o = lax.dot_general(s_t.astype(BF16), v, _TN, preferred_element_type=F32)
    qin = (q * jnp.exp(b)).astype(BF16)
    o = o + lax.dot_general(qin, st.astype(BF16), _NT, preferred_element_type=F32)
    kout = (k * jnp.exp(tot - b)).astype(BF16)
    u = lax.dot_general(v, kout, _TN, preferred_element_type=F32)
    return o, st * jnp.exp(tot) + u


def _scan_kernel(qf_ref, kf_ref, vf_ref, gf_ref, qb_ref, kb_ref, vb_ref, gb_ref,
                 of_ref, ob_ref, sf_ref, sb_ref, row_ref, *, heads, dk, dv):
    c = qf_ref.shape[0]

    @pl.when(pl.program_id(1) == 0)
    def _():
        sf_ref[...] = jnp.zeros_like(sf_ref)
        sb_ref[...] = jnp.zeros_like(sb_ref)

    row = lax.broadcasted_iota(jnp.int32, (c, c), 0)
    col = lax.broadcasted_iota(jnp.int32, (c, c), 1)
    lower = row >= col
    upper = row <= col
    bf_all = _cumsum_rows(jnp.where(lower, 1.0, 0.0).astype(BF16), gf_ref[...])
    bb_all = _cumsum_rows(jnp.where(upper, 1.0, 0.0).astype(BF16), gb_ref[...])
    tot_f = bf_all[c - 1:c, :]
    tot_b = bb_all[0:1, :]
    span = jnp.maximum(jnp.max(-tot_f), jnp.max(-tot_b))
    fast = span < 2.0 * MAX_HALF_RANGE

    def run(step_f, step_b):
        for h in range(heads):
            ks = slice(h * dk, (h + 1) * dk)
            vs = slice(h * dv, (h + 1) * dv)
            o, s_new = step_f(qf_ref[:, ks], kf_ref[:, ks], vf_ref[:, vs], bf_all[:, ks], tot_f[:, ks], sf_ref[h])
            of_ref[:, vs] = o.astype(BF16)
            sf_ref[h] = s_new
            o, s_new = step_b(qb_ref[:, ks], kb_ref[:, ks], vb_ref[:, vs], bb_all[:, ks], tot_b[:, ks], sb_ref[h])
            ob_ref[:, vs] = o.astype(BF16)
            sb_ref[h] = s_new

    @pl.when(fast)
    def _():
        run(functools.partial(_dir_fast, mask=lower), functools.partial(_dir_fast, mask=upper))

    @pl.when(jnp.logical_not(fast))
    def _():
        run(functools.partial(_dir_slow, reverse=False, row_ref=row_ref),
            functools.partial(_dir_slow, reverse=True, row_ref=row_ref))


def _scan(q, kf, kb, v, gf, gb, *, batch, seq, heads, dk, dv):
    n = batch * seq
    c = min(CHUNK, seq)
    nc = seq // c
    kw, vw = heads * dk, heads * dv
    fwd = lambda blk: (lambda b, i: (b * nc + i, blk))
    bwd = lambda blk: (lambda b, i: (b * nc + (nc - 1 - i), blk))
    out_f = lambda b, i: (b * nc + i, 0)
    out_b = lambda b, i: (b * nc + (nc - 1 - i), 0)
    return pl.pallas_call(
        functools.partial(_scan_kernel, heads=heads, dk=dk, dv=dv),
        grid=(batch, nc),
        in_specs=[
            pl.BlockSpec((c, kw), fwd(q[1])), pl.BlockSpec((c, kw), fwd(kf[1])),
            pl.BlockSpec((c, vw), fwd(v[1])), pl.BlockSpec((c, kw), fwd(gf[1])),
            pl.BlockSpec((c, kw), bwd(q[1])), pl.BlockSpec((c, kw), bwd(kb[1])),
            pl.BlockSpec((c, vw), bwd(v[1])), pl.BlockSpec((c, kw), bwd(gb[1])),
        ],
        out_specs=[pl.BlockSpec((c, vw), out_f), pl.BlockSpec((c, vw), out_b)],
        out_shape=[jax.ShapeDtypeStruct((n, vw), BF16), jax.ShapeDtypeStruct((n, vw), BF16)],
        scratch_shapes=[
            pltpu.VMEM((heads, dv, dk), F32),
            pltpu.VMEM((heads, dv, dk), F32),
            pltpu.VMEM((2, c, dk), F32),
        ],
        compiler_params=pltpu.CompilerParams(
            dimension_semantics=("arbitrary", "arbitrary"), vmem_limit_bytes=VMEM_LIMIT),
        name=f"scan_h{heads}",
    )(q[0], kf[0], v[0], gf[0], q[0], kb[0], v[0], gb[0])


def _outproj_kernel(ofg_ref, obg_ref, ofh_ref, obh_ref, gog_ref, hog_ref, h_ref, w_ref,
                    gng_ref, hng_ref, lng_ref, lnb_ref, o_ref, y_ref):
    def head_group(of_ref, ob_ref, gate_ref, g_ref, heads, dv, base):
        for hh in range(heads):
            sl = slice(hh * dv, (hh + 1) * dv)
            o = of_ref[:, sl].astype(F32) + ob_ref[:, sl].astype(F32)
            ms = jnp.mean(o * o, axis=-1, keepdims=True)
            y = o * lax.rsqrt(ms + RMS_EPS) * g_ref[...]
            gate = gate_ref[:, sl].astype(F32)
            y = y * (gate * (1.0 / (1.0 + jnp.exp(-gate))))
            y_ref[:, base + hh * dv: base + (hh + 1) * dv] = y.astype(BF16)

    head_group(ofg_ref, obg_ref, gog_ref, gng_ref, GLA_HEADS, GLA_DV, 0)
    head_group(ofh_ref, obh_ref, hog_ref, hng_ref, HG_HEADS, HG_DIM, GLA_WIDTH)
    m = jnp.dot(y_ref[...], w_ref[...], preferred_element_type=F32)
    o_ref[...] = _layernorm(ALPHA * h_ref[...] + m, lng_ref[...], lnb_ref[...])


def _outproj(ofg, obg, ofh, obh, z, h, w_out, gng, hng, lng, lnb):
    n = h.shape[0]
    tm = min(OUT_TM, n)
    half = lambda i: (i, 0)
    const = lambda i: (0, 0)
    return pl.pallas_call(
        _outproj_kernel,
        grid=(n // tm,),
        in_specs=[
            pl.BlockSpec((tm, GLA_WIDTH), half), pl.BlockSpec((tm, GLA_WIDTH), half),
            pl.BlockSpec((tm, HG_WIDTH), half), pl.BlockSpec((tm, HG_WIDTH), half),
            pl.BlockSpec((tm, GLA_WIDTH), lambda i: (i, 2048 // GLA_WIDTH)),
            pl.BlockSpec((tm, HG_WIDTH), lambda i: (i, 7168 // HG_WIDTH)),
            pl.BlockSpec((tm, D_MODEL), half),
            pl.BlockSpec((D_MODEL, D_MODEL), const, pipeline_mode=pl.Buffered(1)),
            pl.BlockSpec((1, GLA_DV), const), pl.BlockSpec((1, HG_DIM), const),
            pl.BlockSpec((1, D_MODEL), const), pl.BlockSpec((1, D_MODEL), const),
        ],
        out_specs=pl.BlockSpec((tm, D_MODEL), half),
        out_shape=jax.ShapeDtypeStruct((n, D_MODEL), F32),
        scratch_shapes=[pltpu.VMEM((tm, D_MODEL), BF16)],
        compiler_params=pltpu.CompilerParams(
            dimension_semantics=("arbitrary",), vmem_limit_bytes=VMEM_LIMIT),
        name="outproj_ln1",
    )(ofg, obg, ofh, obh, z, z, h, w_out, gng, hng, lng, lnb)


def _mlp_kernel(h_ref, wup_ref, wdn_ref, lng_ref, lnb_ref, o_ref, xb_ref):
    j = pl.program_id(1)

    @pl.when(j == 0)
    def _():
        xb_ref[...] = h_ref[...].astype(BF16)
        o_ref[...] = jnp.zeros_like(o_ref)

    a = jnp.dot(xb_ref[...], wup_ref[...], preferred_element_type=F32)
    a = jnp.square(jnp.maximum(a, 0.0)).astype(BF16)
    o_ref[...] += jnp.dot(a, wdn_ref[...], preferred_element_type=F32)

    @pl.when(j == pl.num_programs(1) - 1)
    def _():
        o_ref[...] = _layernorm(ALPHA * h_ref[...] + o_ref[...], lng_ref[...], lnb_ref[...])


def _mlp(h, w_up, w_down, lng, lnb):
    n = h.shape[0]
    tm = min(MLP_TM, n)
    return pl.pallas_call(
        _mlp_kernel,
        grid=(n // tm, D_FF // MLP_TF),
        in_specs=[
            pl.BlockSpec((tm, D_MODEL), lambda i, j: (i, 0)),
            pl.BlockSpec((D_MODEL, MLP_TF), lambda i, j: (0, j)),
            pl.BlockSpec((MLP_TF, D_MODEL), lambda i, j: (j, 0)),
            pl.BlockSpec((1, D_MODEL), lambda i, j: (0, 0)),
            pl.BlockSpec((1, D_MODEL), lambda i, j: (0, 0)),
        ],
        out_specs=pl.BlockSpec((tm, D_MODEL), lambda i, j: (i, 0)),
        out_shape=jax.ShapeDtypeStruct((n, D_MODEL), F32),
        scratch_shapes=[pltpu.VMEM((tm, D_MODEL), BF16)],
        compiler_params=pltpu.CompilerParams(
            dimension_semantics=("arbitrary", "arbitrary"), vmem_limit_bytes=MLP_VMEM_LIMIT),
        name="mlp_ln2",
    )(h, w_up, w_down, lng, lnb)


def _lower_bounds(p):
    c = jnp.cumsum(jax.nn.softmax(p.astype(F32), axis=0), axis=0)
    return c - c[0:1]


def _prep_layer(l, w_in, gla_w_lr2, gla_b_lr, gla_norm_g, hg_norm_g, lbs_f, lbs_b,
                w_out, ln1_g, ln1_b, w_up, w_down, ln2_g, ln2_b):
    w = w_in[l]
    glr0 = 2 * GLA_KWIDTH + 2 * GLA_WIDTH
    w_main = jnp.concatenate([w[:, :glr0], w[:, glr0 + 2 * GLA_RANK:]], axis=1).astype(BF16)
    w_glr = jnp.pad(w[:, glr0:glr0 + 2 * GLA_RANK], ((0, 0), (0, GLR_PAD - 2 * GLA_RANK))).astype(BF16)
    w2 = gla_w_lr2[l].astype(F32)
    w2c = jnp.zeros((GLR_PAD, 2 * GLA_KWIDTH), F32)
    w2c = w2c.at[:GLA_RANK, :GLA_KWIDTH].set(w2[0]).at[GLA_RANK:2 * GLA_RANK, GLA_KWIDTH:].set(w2[1])
    b2c = gla_b_lr[l].astype(F32).reshape(1, 2 * GLA_KWIDTH)
    lb = jnp.concatenate([lbs_f[l], lbs_b[l]])[None, :]
    gate_c = jnp.concatenate([lb, 1.0 - lb, jnp.log(lb), jnp.log1p(-lb), jnp.zeros((4, 2 * HG_WIDTH), F32)], axis=0)
    row = lambda a: a[l].astype(F32)[None, :]
    return dict(
        w_main=w_main, w_glr=w_glr, w2c=w2c.astype(BF16), b2c=b2c, gate_c=gate_c,
        w_out=w_out[l].astype(BF16), gng=row(gla_norm_g), hng=row(hg_norm_g),
        ln1_g=row(ln1_g), ln1_b=row(ln1_b),
        w_up=w_up[l].astype(BF16), w_down=w_down[l].astype(BF16),
        ln2_g=row(ln2_g), ln2_b=row(ln2_b),
    )


def _trunk(x, layers):
    batch, seq, _ = x.shape
    h = x.reshape(batch * seq, D_MODEL).astype(F32)
    for p in layers:
        z, logf, ga = _inproj(h, p["w_main"], p["w_glr"], p["w2c"], p["b2c"], p["gate_c"])
        ofg, obg = _scan((z, 0), (z, 1), (z, 1), (z, 1), (ga, 0), (ga, 1),
                         batch=batch, seq=seq, heads=GLA_HEADS, dk=GLA_DK, dv=GLA_DV)
        ofh, obh = _scan((z, 3), (z, 4), (z, 5), (z, 6), (logf, 0), (logf, 1),
                         batch=batch, seq=seq, heads=HG_HEADS, dk=HG_DIM, dv=HG_DIM)
        h = _outproj(ofg, obg, ofh, obh, z, h, p["w_out"], p["gng"], p["hng"], p["ln1_g"], p["ln1_b"])
        h = _mlp(h, p["w_up"], p["w_down"], p["ln2_g"], p["ln2_b"])
    return h.reshape(batch, seq, D_MODEL).astype(x.dtype)


def kernel(x_prompt, x_sample, w_in, gla_w_lr2, gla_b_lr, gla_norm_g, hg_norm_g, lower_bounds,
           w_out, ln1_g, ln1_b, w_up, w_down, ln2_g, ln2_b):
    lbs_f = _lower_bounds(lower_bounds[0])
    lbs_b = _lower_bounds(lower_bounds[1])
    layers = [_prep_layer(l, w_in, gla_w_lr2, gla_b_lr, gla_norm_g, hg_norm_g, lbs_f, lbs_b,
                          w_out, ln1_g, ln1_b, w_up, w_down, ln2_g, ln2_b) for l in range(DEPTH)]
    return (_trunk(x_prompt, layers), _trunk(x_sample, layers))
```

```python
import functools

import jax
import jax.numpy as jnp
from jax import lax
from jax.experimental import pallas as pl
from jax.experimental.pallas import tpu as pltpu

F32 = jnp.float32
BF16 = jnp.bfloat16

D_MODEL = 2048
DEPTH = 2
GLA_HEADS = 4
GLA_WIDTH = D_MODEL // 2
GLA_DV = GLA_WIDTH // GLA_HEADS
GLA_DK = GLA_DV // 2
GLA_KWIDTH = GLA_HEADS * GLA_DK
GLA_RANK = 16
GLA_TAU = 16.0
HG_DIM = 128
HG_WIDTH = D_MODEL - GLA_WIDTH
HG_HEADS = HG_WIDTH // HG_DIM
D_FF = 4 * D_MODEL
LN_EPS = 1e-5
RMS_EPS = 1e-6
F32_TINY = 1.1754943508222875e-38
ALPHA = (2.0 * DEPTH) ** 0.25

LANES = 128
GLR_PAD = LANES
Z_WIDTH = 8192
VMEM_LIMIT = 48 * 1024 * 1024

IN_TN = 512
IN_TM = 1024
MXU_COLS = 256
GATE_TILE_LO = 4096 // IN_TN
GATE_TILE_HI = 6144 // IN_TN

CHUNK = 128
MAX_HALF_RANGE = 80.0

OUT_TM = 512
MLP_TM = 1024
MLP_TF = 512
MLP_VMEM_LIMIT = 58 * 1024 * 1024

_NT = (((1,), (1,)), ((), ()))
_TN = (((0,), (0,)), ((), ()))


def _log_sigmoid(x):
    return jnp.minimum(x, 0.0) - jnp.log(1.0 + jnp.exp(-jnp.abs(x)))


def _layernorm(r, g, b):
    mu = jnp.mean(r, axis=-1, keepdims=True)
    d = r - mu
    var = jnp.mean(d * d, axis=-1, keepdims=True)
    return d * lax.rsqrt(var + LN_EPS) * g + b


def _inproj_kernel(x_ref, w_ref, wglr_ref, w2_ref, b2_ref, gc_ref, z_ref, logf_ref, ga_ref, xb_ref):
    j = pl.program_id(1)
    is_gate = jnp.logical_and(j >= GATE_TILE_LO, j < GATE_TILE_HI)
    col_groups = lambda width: [slice(c, c + MXU_COLS) for c in range(0, width, MXU_COLS)]

    @pl.when(j == 0)
    def _():
        xb = x_ref[...].astype(BF16)
        xb_ref[...] = xb
        glr = jnp.dot(xb, wglr_ref[...], preferred_element_type=F32).astype(BF16)
        for cols in col_groups(ga_ref.shape[1]):
            a = jnp.dot(glr, w2_ref[:, cols], preferred_element_type=F32) + b2_ref[:, cols]
            ga_ref[:, cols] = _log_sigmoid(a) * (1.0 / GLA_TAU)
        for cols in col_groups(z_ref.shape[1]):
            q = jnp.dot(xb, w_ref[:, cols], preferred_element_type=F32) * (GLA_DK ** -0.5)
            z_ref[:, cols] = q.astype(BF16)

    @pl.when(jnp.logical_and(j > 0, jnp.logical_not(is_gate)))
    def _():
        z_ref[...] = jnp.dot(xb_ref[...], w_ref[...], preferred_element_type=F32).astype(BF16)

    @pl.when(is_gate)
    def _():
        f_min = None
        half = x_ref.shape[0] // 2
        for cols in col_groups(z_ref.shape[1]):
            lb = gc_ref[0:1, cols]
            one_minus_lb = gc_ref[1:2, cols]
            for rows in (slice(0, half), slice(half, 2 * half)):
                acc = jnp.dot(xb_ref[rows, :], w_ref[:, cols], preferred_element_type=F32)
                e = jnp.exp(-jnp.abs(acc))
                r = 1.0 / (1.0 + e)
                er = e * r
                pos = acc >= 0.0
                f = lb + one_minus_lb * jnp.where(pos, r, er)
                logf_ref[rows, cols] = jnp.log(f)
                z_ref[rows, cols] = (one_minus_lb * jnp.where(pos, er, r)).astype(BF16)
                f_min = f if f_min is None else jnp.minimum(f_min, f)

        @pl.when(jnp.logical_not(jnp.min(f_min) >= F32_TINY))
        def _():
            acc = jnp.dot(xb_ref[...], w_ref[...], preferred_element_type=F32)
            log_lb = gc_ref[2:3, :]
            x2 = gc_ref[3:4, :] + _log_sigmoid(acc)
            logf_ref[...] = (jnp.maximum(log_lb, x2)
                             + jnp.log(1.0 + jnp.exp(-jnp.abs(log_lb - x2))))


def _inproj(x, w_main, w_glr, w2c, b2c, gate_c):
    n = x.shape[0]
    tm = min(IN_TM, n)
    n_gate = GATE_TILE_HI - GATE_TILE_LO
    gate_idx = lambda i, j: jnp.clip(j - GATE_TILE_LO, 0, n_gate - 1)
    return pl.pallas_call(
        _inproj_kernel,
        grid=(n // tm, Z_WIDTH // IN_TN),
        in_specs=[
            pl.BlockSpec((tm, D_MODEL), lambda i, j: (i, 0)),
            pl.BlockSpec((D_MODEL, IN_TN), lambda i, j: (0, j)),
            pl.BlockSpec((D_MODEL, GLR_PAD), lambda i, j: (0, 0)),
            pl.BlockSpec((GLR_PAD, 2 * GLA_KWIDTH), lambda i, j: (0, 0)),
            pl.BlockSpec((1, 2 * GLA_KWIDTH), lambda i, j: (0, 0)),
            pl.BlockSpec((8, IN_TN), lambda i, j: (0, gate_idx(i, j))),
        ],
        out_specs=[
            pl.BlockSpec((tm, IN_TN), lambda i, j: (i, j)),
            pl.BlockSpec((tm, IN_TN), lambda i, j: (i, gate_idx(i, j))),
            pl.BlockSpec((tm, 2 * GLA_KWIDTH), lambda i, j: (i, 0)),
        ],
        out_shape=[
            jax.ShapeDtypeStruct((n, Z_WIDTH), BF16),
            jax.ShapeDtypeStruct((n, 2 * HG_WIDTH), F32),
            jax.ShapeDtypeStruct((n, 2 * GLA_KWIDTH), F32),
        ],
        scratch_shapes=[pltpu.VMEM((tm, D_MODEL), BF16)],
        compiler_params=pltpu.CompilerParams(
            dimension_semantics=("arbitrary", "arbitrary"), vmem_limit_bytes=VMEM_LIMIT),
        name="inproj",
    )(x, w_main, w_glr, w2c, b2c, gate_c)


def _cumsum_rows(tri, g):
    hi = g.astype(BF16)
    lo = (g - hi.astype(F32)).astype(BF16)
    return (jnp.dot(tri, hi, preferred_element_type=F32) + jnp.dot(tri, lo, preferred_element_type=F32))


def _dir_fast(q, k, v, b, tot, st, mask):
    m = 0.5 * tot
    qt = q * jnp.exp(b - m).astype(BF16)
    kt = k * jnp.exp(m - b).astype(BF16)
    s = lax.dot_general(qt, kt, _NT, preferred_element_type=F32)
    s = jnp.where(mask, s, 0.0).astype(BF16)
    o = jnp.dot(s, v, preferred_element_type=F32)
    ssc = (st * jnp.exp(m)).astype(BF16)
    o = o + lax.dot_general(qt, ssc, _NT, preferred_element_type=F32)
    u = lax.dot_general(v, kt, _TN, preferred_element_type=F32)
    return o, st * jnp.exp(tot) + u * jnp.exp(tot - m)


def _dir_slow(q, k, v, b, tot, st, reverse, row_ref):
    c = q.shape[0]
    q = q.astype(F32)
    k = k.astype(F32)
    row_ref[0] = b
    row_ref[1] = q
    jj = lax.broadcasted_iota(jnp.int32, (c, 1), 0)
    lane = lax.broadcasted_iota(jnp.int32, (c, c), 1)

    def body(i, s_t):
        bi = row_ref[0, pl.ds(i, 1), :]
        qi = row_ref[1, pl.ds(i, 1), :]
        valid = (jj >= i) if reverse else (jj <= i)
        w = jnp.where(valid, jnp.exp(jnp.minimum(bi - b, 0.0)), 0.0)
        col = jnp.sum(qi * k * w, axis=-1, keepdims=True)
        return jnp.where(lane == i, col, s_t)

    s_t = lax.fori_loop(0, c, body, jnp.zeros((c, c), F32))
    o = lax.dot_general(s_t.astype(BF16), v, _TN, preferred_element_type=F32)
    qin = (q * jnp.exp(b)).astype(BF16)
    o = o + lax.dot_general(qin, st.astype(BF16), _NT, preferred_element_type=F32)
    kout = (k * jnp.exp(tot - b)).astype(BF16)
    u = lax.dot_general(v, kout, _TN, preferred_element_type=F32)
    return o, st * jnp.exp(tot) + u


def _scan_kernel(qf_ref, kf_ref, vf_ref, gf_ref, qb_ref, kb_ref, vb_ref, gb_ref,
                 of_ref, ob_ref, sf_ref, sb_ref, row_ref, *, heads, dk, dv):
    c = qf_ref.shape[0]

    @pl.when(pl.program_id(1) == 0)
    def _():
        sf_ref[...] = jnp.zeros_like(sf_ref)
        sb_ref[...] = jnp.zeros_like(sb_ref)

    row = lax.broadcasted_iota(jnp.int32, (c, c), 0)
    col = lax.broadcasted_iota(jnp.int32, (c, c), 1)
    lower = row >= col
    upper = row <= col
    bf_all = _cumsum_rows(jnp.where(lower, 1.0, 0.0).astype(BF16), gf_ref[...])
    bb_all = _cumsum_rows(jnp.where(upper, 1.0, 0.0).astype(BF16), gb_ref[...])
    tot_f = bf_all[c - 1:c, :]
    tot_b = bb_all[0:1, :]
    span = jnp.maximum(jnp.max(-tot_f), jnp.max(-tot_b))
    fast = span < 2.0 * MAX_HALF_RANGE

    def run(step_f, step_b):
        for h in range(heads):
            ks = slice(h * dk, (h + 1) * dk)
            vs = slice(h * dv, (h + 1) * dv)
            o, s_new = step_f(qf_ref[:, ks], kf_ref[:, ks], vf_ref[:, vs], bf_all[:, ks], tot_f[:, ks], sf_ref[h])
            of_ref[:, vs] = o.astype(BF16)
            sf_ref[h] = s_new
            o, s_new = step_b(qb_ref[:, ks], kb_ref[:, ks], vb_ref[:, vs], bb_all[:, ks], tot_b[:, ks], sb_ref[h])
            ob_ref[:, vs] = o.astype(BF16)
            sb_ref[h] = s_new

    @pl.when(fast)
    def _():
        run(functools.partial(_dir_fast, mask=lower), functools.partial(_dir_fast, mask=upper))

    @pl.when(jnp.logical_not(fast))
    def _():
        run(functools.partial(_dir_slow, reverse=False, row_ref=row_ref),
            functools.partial(_dir_slow, reverse=True, row_ref=row_ref))


def _scan(q, kf, kb, v, gf, gb, *, batch, seq, heads, dk, dv):
    n = batch * seq
    c = min(CHUNK, seq)
    nc = seq // c
    kw, vw = heads * dk, heads * dv
    fwd = lambda blk: (lambda b, i: (b * nc + i, blk))
    bwd = lambda blk: (lambda b, i: (b * nc + (nc - 1 - i), blk))
    out_f = lambda b, i: (b * nc + i, 0)
    out_b = lambda b, i: (b * nc + (nc - 1 - i), 0)
    return pl.pallas_call(
        functools.partial(_scan_kernel, heads=heads, dk=dk, dv=dv),
        grid=(batch, nc),
        in_specs=[
            pl.BlockSpec((c, kw), fwd(q[1])), pl.BlockSpec((c, kw), fwd(kf[1])),
            pl.BlockSpec((c, vw), fwd(v[1])), pl.BlockSpec((c, kw), fwd(gf[1])),
            pl.BlockSpec((c, kw), bwd(q[1])), pl.BlockSpec((c, kw), bwd(kb[1])),
            pl.BlockSpec((c, vw), bwd(v[1])), pl.BlockSpec((c, kw), bwd(gb[1])),
        ],
        out_specs=[pl.BlockSpec((c, vw), out_f), pl.BlockSpec((c, vw), out_b)],
        out_shape=[jax.ShapeDtypeStruct((n, vw), BF16), jax.ShapeDtypeStruct((n, vw), BF16)],
        scratch_shapes=[
            pltpu.VMEM((heads, dv, dk), F32),
            pltpu.VMEM((heads, dv, dk), F32),
            pltpu.VMEM((2, c, dk), F32),
        ],
        compiler_params=pltpu.CompilerParams(
            dimension_semantics=("arbitrary", "arbitrary"), vmem_limit_bytes=VMEM_LIMIT),
        name=f"scan_h{heads}",
    )(q[0], kf[0], v[0], gf[0], q[0], kb[0], v[0], gb[0])


def _outproj_kernel(ofg_ref, obg_ref, ofh_ref, obh_ref, gog_ref, hog_ref, h_ref, w_ref,
                    gng_ref, hng_ref, lng_ref, lnb_ref, o_ref, y_ref):
    def head_group(of_ref, ob_ref, gate_ref, g_ref, heads, dv, base):
        for hh in range(heads):
            sl = slice(hh * dv, (hh + 1) * dv)
            o = of_ref[:, sl].astype(F32) + ob_ref[:, sl].astype(F32)
            ms = jnp.mean(o * o, axis=-1, keepdims=True)
            y = o * lax.rsqrt(ms + RMS_EPS) * g_ref[...]
            gate = gate_ref[:, sl].astype(F32)
            y = y * (gate * (1.0 / (1.0 + jnp.exp(-gate))))
            y_ref[:, base + hh * dv: base + (hh + 1) * dv] = y.astype(BF16)

    head_group(ofg_ref, obg_ref, gog_ref, gng_ref, GLA_HEADS, GLA_DV, 0)
    head_group(ofh_ref, obh_ref, hog_ref, hng_ref, HG_HEADS, HG_DIM, GLA_WIDTH)
    m = jnp.dot(y_ref[...], w_ref[...], preferred_element_type=F32)
    o_ref[...] = _layernorm(ALPHA * h_ref[...] + m, lng_ref[...], lnb_ref[...])


def _outproj(ofg, obg, ofh, obh, z, h, w_out, gng, hng, lng, lnb):
    n = h.shape[0]
    tm = min(OUT_TM, n)
    half = lambda i: (i, 0)
    const = lambda i: (0, 0)
    return pl.pallas_call(
        _outproj_kernel,
        grid=(n // tm,),
        in_specs=[
            pl.BlockSpec((tm, GLA_WIDTH), half), pl.BlockSpec((tm, GLA_WIDTH), half),
            pl.BlockSpec((tm, HG_WIDTH), half), pl.BlockSpec((tm, HG_WIDTH), half),
            pl.BlockSpec((tm, GLA_WIDTH), lambda i: (i, 2048 // GLA_WIDTH)),
            pl.BlockSpec((tm, HG_WIDTH), lambda i: (i, 7168 // HG_WIDTH)),
            pl.BlockSpec((tm, D_MODEL), half),
            pl.BlockSpec((D_MODEL, D_MODEL), const, pipeline_mode=pl.Buffered(1)),
            pl.BlockSpec((1, GLA_DV), const), pl.BlockSpec((1, HG_DIM), const),
            pl.BlockSpec((1, D_MODEL), const), pl.BlockSpec((1, D_MODEL), const),
        ],
        out_specs=pl.BlockSpec((tm, D_MODEL), half),
        out_shape=jax.ShapeDtypeStruct((n, D_MODEL), F32),
        scratch_shapes=[pltpu.VMEM((tm, D_MODEL), BF16)],
        compiler_params=pltpu.CompilerParams(
            dimension_semantics=("arbitrary",), vmem_limit_bytes=VMEM_LIMIT),
        name="outproj_ln1",
    )(ofg, obg, ofh, obh, z, z, h, w_out, gng, hng, lng, lnb)


def _mlp_kernel(h_ref, wup_ref, wdn_ref, lng_ref, lnb_ref, o_ref, xb_ref):
    j = pl.program_id(1)

    @pl.when(j == 0)
    def _():
        xb_ref[...] = h_ref[...].astype(BF16)
        o_ref[...] = jnp.zeros_like(o_ref)

    a = jnp.dot(xb_ref[...], wup_ref[...], preferred_element_type=F32)
    a = jnp.square(jnp.maximum(a, 0.0)).astype(BF16)
    o_ref[...] += jnp.dot(a, wdn_ref[...], preferred_element_type=F32)

    @pl.when(j == pl.num_programs(1) - 1)
    def _():
        o_ref[...] = _layernorm(ALPHA * h_ref[...] + o_ref[...], lng_ref[...], lnb_ref[...])


def _mlp(h, w_up, w_down, lng, lnb):
    n = h.shape[0]
    tm = min(MLP_TM, n)
    return pl.pallas_call(
        _mlp_kernel,
        grid=(n // tm, D_FF // MLP_TF),
        in_specs=[
            pl.BlockSpec((tm, D_MODEL), lambda i, j: (i, 0)),
            pl.BlockSpec((D_MODEL, MLP_TF), lambda i, j: (0, j)),
            pl.BlockSpec((MLP_TF, D_MODEL), lambda i, j: (j, 0)),
            pl.BlockSpec((1, D_MODEL), lambda i, j: (0, 0)),
            pl.BlockSpec((1, D_MODEL), lambda i, j: (0, 0)),
        ],
        out_specs=pl.BlockSpec((tm, D_MODEL), lambda i, j: (i, 0)),
        out_shape=jax.ShapeDtypeStruct((n, D_MODEL), F32),
        scratch_shapes=[pltpu.VMEM((tm, D_MODEL), BF16)],
        compiler_params=pltpu.CompilerParams(
            dimension_semantics=("arbitrary", "arbitrary"), vmem_limit_bytes=MLP_VMEM_LIMIT),
        name="mlp_ln2",
    )(h, w_up, w_down, lng, lnb)


def _lower_bounds(p):
    c = jnp.cumsum(jax.nn.softmax(p.astype(F32), axis=0), axis=0)
    return c - c[0:1]


def _prep_layer(l, w_in, gla_w_lr2, gla_b_lr, gla_norm_g, hg_norm_g, lbs_f, lbs_b,
                w_out, ln1_g, ln1_b, w_up, w_down, ln2_g, ln2_b):
    w = w_in[l]
    glr0 = 2 * GLA_KWIDTH + 2 * GLA_WIDTH
    w_main = jnp.concatenate([w[:, :glr0], w[:, glr0 + 2 * GLA_RANK:]], axis=1).astype(BF16)
    w_glr = jnp.pad(w[:, glr0:glr0 + 2 * GLA_RANK], ((0, 0), (0, GLR_PAD - 2 * GLA_RANK))).astype(BF16)
    w2 = gla_w_lr2[l].astype(F32)
    w2c = jnp.zeros((GLR_PAD, 2 * GLA_KWIDTH), F32)
    w2c = w2c.at[:GLA_RANK, :GLA_KWIDTH].set(w2[0]).at[GLA_RANK:2 * GLA_RANK, GLA_KWIDTH:].set(w2[1])
    b2c = gla_b_lr[l].astype(F32).reshape(1, 2 * GLA_KWIDTH)
    lb = jnp.concatenate([lbs_f[l], lbs_b[l]])[None, :]
    gate_c = jnp.concatenate([lb, 1.0 - lb, jnp.log(lb), jnp.log1p(-lb), jnp.zeros((4, 2 * HG_WIDTH), F32)], axis=0)
    row = lambda a: a[l].astype(F32)[None, :]
    return dict(
        w_main=w_main, w_glr=w_glr, w2c=w2c.astype(BF16), b2c=b2c, gate_c=gate_c,
        w_out=w_out[l].astype(BF16), gng=row(gla_norm_g), hng=row(hg_norm_g),
        ln1_g=row(ln1_g), ln1_b=row(ln1_b),
        w_up=w_up[l].astype(BF16), w_down=w_down[l].astype(BF16),
        ln2_g=row(ln2_g), ln2_b=row(ln2_b),
    )


def _trunk(x, layers):
    batch, seq, _ = x.shape
    h = x.reshape(batch * seq, D_MODEL).astype(F32)
    for p in layers:
        z, logf, ga = _inproj(h, p["w_main"], p["w_glr"], p["w2c"], p["b2c"], p["gate_c"])
        ofg, obg = _scan((z, 0), (z, 1), (z, 1), (z, 1), (ga, 0), (ga, 1),
                         batch=batch, seq=seq, heads=GLA_HEADS, dk=GLA_DK, dv=GLA_DV)
        ofh, obh = _scan((z, 3), (z, 4), (z, 5), (z, 6), (logf, 0), (logf, 1),
                         batch=batch, seq=seq, heads=HG_HEADS, dk=HG_DIM, dv=HG_DIM)
        h = _outproj(ofg, obg, ofh, obh, z, h, p["w_out"], p["gng"], p["hng"], p["ln1_g"], p["ln1_b"])
        h = _mlp(h, p["w_up"], p["w_down"], p["ln2_g"], p["ln2_b"])
    return h.reshape(batch, seq, D_MODEL).astype(x.dtype)


def kernel(x_prompt, x_sample, w_in, gla_w_lr2, gla_b_lr, gla_norm_g, hg_norm_g, lower_bounds,
           w_out, ln1_g, ln1_b, w_up, w_down, ln2_g, ln2_b):
    lbs_f = _lower_bounds(lower_bounds[0])
    lbs_b = _lower_bounds(lower_bounds[1])
    layers = [_prep_layer(l, w_in, gla_w_lr2, gla_b_lr, gla_norm_g, hg_norm_g, lbs_f, lbs_b,
                          w_out, ln1_g, ln1_b, w_up, w_down, ln2_g, ln2_b) for l in range(DEPTH)]
    return (_trunk(x_prompt, layers), _trunk(x_sample, layers))
```

```python
import functools

import jax
import jax.numpy as jnp
from jax import lax
from jax.experimental import pallas as pl
from jax.experimental.pallas import tpu as pltpu

F32 = jnp.float32
BF16 = jnp.bfloat16

D_MODEL = 2048
DEPTH = 2
GLA_HEADS = 4
GLA_WIDTH = D_MODEL // 2
GLA_DV = GLA_WIDTH // GLA_HEADS
GLA_DK = GLA_DV // 2
GLA_KWIDTH = GLA_HEADS * GLA_DK
GLA_RANK = 16
GLA_TAU = 16.0
HG_DIM = 128
HG_WIDTH = D_MODEL - GLA_WIDTH
HG_HEADS = HG_WIDTH // HG_DIM
D_FF = 4 * D_MODEL
LN_EPS = 1e-5
RMS_EPS = 1e-6
F32_TINY = 1.1754943508222875e-38
ALPHA = (2.0 * DEPTH) ** 0.25

LANES = 128
GLR_PAD = LANES
Z_WIDTH = 8192
VMEM_LIMIT = 48 * 1024 * 1024

IN_TN = 512
IN_TM = 1024
MXU_COLS = 256
GATE_TILE_LO = 4096 // IN_TN
GATE_TILE_HI = 6144 // IN_TN

GLA_CHUNK = 256
HG_CHUNK = 128
MAX_HALF_RANGE = 80.0

OUT_TM = 512
MLP_TM = 1024
MLP_TF = 512
MLP_VMEM_LIMIT = 58 * 1024 * 1024

_NT = (((1,), (1,)), ((), ()))
_TN = (((0,), (0,)), ((), ()))


def _log_sigmoid(x):
    return jnp.minimum(x, 0.0) - jnp.log(1.0 + jnp.exp(-jnp.abs(x)))


def _layernorm(r, g, b):
    mu = jnp.mean(r, axis=-1, keepdims=True)
    d = r - mu
    var = jnp.mean(d * d, axis=-1, keepdims=True)
    return d * lax.rsqrt(var + LN_EPS) * g + b


def _inproj_kernel(x_ref, w_ref, wglr_ref, w2_ref, b2_ref, gc_ref, z_ref, logf_ref, ga_ref, xb_ref):
    j = pl.program_id(1)
    is_gate = jnp.logical_and(j >= GATE_TILE_LO, j < GATE_TILE_HI)
    col_groups = lambda width: [slice(c, c + MXU_COLS) for c in range(0, width, MXU_COLS)]

    @pl.when(j == 0)
    def _():
        xb = x_ref[...].astype(BF16)
        xb_ref[...] = xb
        glr = jnp.dot(xb, wglr_ref[...], preferred_element_type=F32).astype(BF16)
        for cols in col_groups(ga_ref.shape[1]):
            a = jnp.dot(glr, w2_ref[:, cols], preferred_element_type=F32) + b2_ref[:, cols]
            ga_ref[:, cols] = _log_sigmoid(a) * (1.0 / GLA_TAU)
        for cols in col_groups(z_ref.shape[1]):
            q = jnp.dot(xb, w_ref[:, cols], preferred_element_type=F32) * (GLA_DK ** -0.5)
            z_ref[:, cols] = q.astype(BF16)

    @pl.when(jnp.logical_and(j > 0, jnp.logical_not(is_gate)))
    def _():
        z_ref[...] = jnp.dot(xb_ref[...], w_ref[...], preferred_element_type=F32).astype(BF16)

    @pl.when(is_gate)
    def _():
        f_min = None
        half = x_ref.shape[0] // 2
        for cols in col_groups(z_ref.shape[1]):
            lb = gc_ref[0:1, cols]
            one_minus_lb = gc_ref[1:2, cols]
            for rows in (slice(0, half), slice(half, 2 * half)):
                acc = jnp.dot(xb_ref[rows, :], w_ref[:, cols], preferred_element_type=F32)
                e = jnp.exp(-jnp.abs(acc))
                r = 1.0 / (1.0 + e)
                er = e * r
                pos = acc >= 0.0
                f = lb + one_minus_lb * jnp.where(pos, r, er)
                logf_ref[rows, cols] = jnp.log(f)
                z_ref[rows, cols] = (one_minus_lb * jnp.where(pos, er, r)).astype(BF16)
                f_min = f if f_min is None else jnp.minimum(f_min, f)

        @pl.when(jnp.logical_not(jnp.min(f_min) >= F32_TINY))
        def _():
            acc = jnp.dot(xb_ref[...], w_ref[...], preferred_element_type=F32)
            log_lb = gc_ref[2:3, :]
            x2 = gc_ref[3:4, :] + _log_sigmoid(acc)
            logf_ref[...] = (jnp.maximum(log_lb, x2)
                             + jnp.log(1.0 + jnp.exp(-jnp.abs(log_lb - x2))))


def _inproj(x, w_main, w_glr, w2c, b2c, gate_c):
    n = x.shape[0]
    tm = min(IN_TM, n)
    n_gate = GATE_TILE_HI - GATE_TILE_LO
    gate_idx = lambda i, j: jnp.clip(j - GATE_TILE_LO, 0, n_gate - 1)
    return pl.pallas_call(
        _inproj_kernel,
        grid=(n // tm, Z_WIDTH // IN_TN),
        in_specs=[
            pl.BlockSpec((tm, D_MODEL), lambda i, j: (i, 0)),
            pl.BlockSpec((D_MODEL, IN_TN), lambda i, j: (0, j)),
            pl.BlockSpec((D_MODEL, GLR_PAD), lambda i, j: (0, 0)),
            pl.BlockSpec((GLR_PAD, 2 * GLA_KWIDTH), lambda i, j: (0, 0)),
            pl.BlockSpec((1, 2 * GLA_KWIDTH), lambda i, j: (0, 0)),
            pl.BlockSpec((8, IN_TN), lambda i, j: (0, gate_idx(i, j))),
        ],
        out_specs=[
            pl.BlockSpec((tm, IN_TN), lambda i, j: (i, j)),
            pl.BlockSpec((tm, IN_TN), lambda i, j: (i, gate_idx(i, j))),
            pl.BlockSpec((tm, 2 * GLA_KWIDTH), lambda i, j: (i, 0)),
        ],
        out_shape=[
            jax.ShapeDtypeStruct((n, Z_WIDTH), BF16),
            jax.ShapeDtypeStruct((n, 2 * HG_WIDTH), F32),
            jax.ShapeDtypeStruct((n, 2 * GLA_KWIDTH), F32),
        ],
        scratch_shapes=[pltpu.VMEM((tm, D_MODEL), BF16)],
        compiler_params=pltpu.CompilerParams(
            dimension_semantics=("arbitrary", "arbitrary"), vmem_limit_bytes=VMEM_LIMIT),
        name="inproj",
    )(x, w_main, w_glr, w2c, b2c, gate_c)


def _cumsum_rows(tri, g):
    hi = g.astype(BF16)
    lo = (g - hi.astype(F32)).astype(BF16)
    return (jnp.dot(tri, hi, preferred_element_type=F32) + jnp.dot(tri, lo, preferred_element_type=F32))


def _dir_fast(q, k, v, b, tot, st, mask):
    m = 0.5 * tot
    qt = q * jnp.exp(b - m).astype(BF16)
    kt = k * jnp.exp(m - b).astype(BF16)
    c = q.shape[0]
    ssc = (st * jnp.exp(m)).astype(BF16)
    r = lax.dot_general(qt, jnp.concatenate([kt, ssc], axis=0), _NT, preferred_element_type=F32)
    s = jnp.where(mask, r[:, :c], 0.0).astype(BF16)
    o = jnp.dot(s, v, preferred_element_type=F32) + r[:, c:]
    u = lax.dot_general(v, kt, _TN, preferred_element_type=F32)
    return o, st * jnp.exp(tot) + u * jnp.exp(tot - m)


def _dir_slow(q, k, v, b, tot, st, reverse, row_ref):
    c = q.shape[0]
    q = q.astype(F32)
    k = k.astype(F32)
    row_ref[0] = b
    row_ref[1] = q
    jj = lax.broadcasted_iota(jnp.int32, (c, 1), 0)
    lane = lax.broadcasted_iota(jnp.int32, (c, c), 1)

    def body(i, s_t):
        bi = row_ref[0, pl.ds(i, 1), :]
        qi = row_ref[1, pl.ds(i, 1), :]
        valid = (jj >= i) if reverse else (jj <= i)
        w = jnp.where(valid, jnp.exp(jnp.minimum(bi - b, 0.0)), 0.0)
        col = jnp.sum(qi * k * w, axis=-1, keepdims=True)
        return jnp.where(lane == i, col, s_t)

    s_t = lax.fori_loop(0, c, body, jnp.zeros((c, c), F32))
    o = lax.dot_general(s_t.astype(BF16), v, _TN, preferred_element_type=F32)
    qin = (q * jnp.exp(b)).astype(BF16)
    o = o + lax.dot_general(qin, st.astype(BF16), _NT, preferred_element_type=F32)
    kout = (k * jnp.exp(tot - b)).astype(BF16)
    u = lax.dot_general(v, kout, _TN, preferred_element_type=F32)
    return o, st * jnp.exp(tot) + u


def _scan_kernel(qf_ref, kf_ref, vf_ref, gf_ref, qb_ref, kb_ref, vb_ref, gb_ref,
                 of_ref, ob_ref, sf_ref, sb_ref, row_ref, *, heads, dk, dv):
    c = qf_ref.shape[0]

    @pl.when(pl.program_id(1) == 0)
    def _():
        sf_ref[...] = jnp.zeros_like(sf_ref)
        sb_ref[...] = jnp.zeros_like(sb_ref)

    row = lax.broadcasted_iota(jnp.int32, (c, c), 0)
    col = lax.broadcasted_iota(jnp.int32, (c, c), 1)
    lower = row >= col
    upper = row <= col
    bf_all = _cumsum_rows(jnp.where(lower, 1.0, 0.0).astype(BF16), gf_ref[...])
    bb_all = _cumsum_rows(jnp.where(upper, 1.0, 0.0).astype(BF16), gb_ref[...])
    tot_f = bf_all[c - 1:c, :]
    tot_b = bb_all[0:1, :]
    span = jnp.maximum(jnp.max(-tot_f), jnp.max(-tot_b))
    fast = span < 2.0 * MAX_HALF_RANGE

    def run(step_f, step_b):
        for h in range(heads):
            ks = slice(h * dk, (h + 1) * dk)
            vs = slice(h * dv, (h + 1) * dv)
            o, s_new = step_f(qf_ref[:, ks], kf_ref[:, ks], vf_ref[:, vs], bf_all[:, ks], tot_f[:, ks], sf_ref[h])
            of_ref[:, vs] = o.astype(BF16)
            sf_ref[h] = s_new
            o, s_new = step_b(qb_ref[:, ks], kb_ref[:, ks], vb_ref[:, vs], bb_all[:, ks], tot_b[:, ks], sb_ref[h])
            ob_ref[:, vs] = o.astype(BF16)
            sb_ref[h] = s_new

    @pl.when(fast)
    def _():
        run(functools.partial(_dir_fast, mask=lower), functools.partial(_dir_fast, mask=upper))

    @pl.when(jnp.logical_not(fast))
    def _():
        run(functools.partial(_dir_slow, reverse=False, row_ref=row_ref),
            functools.partial(_dir_slow, reverse=True, row_ref=row_ref))


def _scan(q, kf, kb, v, gf, gb, *, batch, seq, heads, dk, dv, chunk):
    n = batch * seq
    c = min(chunk, seq)
    nc = seq // c
    kw, vw = heads * dk, heads * dv
    fwd = lambda blk: (lambda b, i: (b * nc + i, blk))
    bwd = lambda blk: (lambda b, i: (b * nc + (nc - 1 - i), blk))
    out_f = lambda b, i: (b * nc + i, 0)
    out_b = lambda b, i: (b * nc + (nc - 1 - i), 0)
    return pl.pallas_call(
        functools.partial(_scan_kernel, heads=heads, dk=dk, dv=dv),
        grid=(batch, nc),
        in_specs=[
            pl.BlockSpec((c, kw), fwd(q[1])), pl.BlockSpec((c, kw), fwd(kf[1])),
            pl.BlockSpec((c, vw), fwd(v[1])), pl.BlockSpec((c, kw), fwd(gf[1])),
            pl.BlockSpec((c, kw), bwd(q[1])), pl.BlockSpec((c, kw), bwd(kb[1])),
            pl.BlockSpec((c, vw), bwd(v[1])), pl.BlockSpec((c, kw), bwd(gb[1])),
        ],
        out_specs=[pl.BlockSpec((c, vw), out_f), pl.BlockSpec((c, vw), out_b)],
        out_shape=[jax.ShapeDtypeStruct((n, vw), BF16), jax.ShapeDtypeStruct((n, vw), BF16)],
        scratch_shapes=[
            pltpu.VMEM((heads, dv, dk), F32),
            pltpu.VMEM((heads, dv, dk), F32),
            pltpu.VMEM((2, c, dk), F32),
        ],
        compiler_params=pltpu.CompilerParams(
            dimension_semantics=("arbitrary", "arbitrary"), vmem_limit_bytes=VMEM_LIMIT),
        name=f"scan_h{heads}",
    )(q[0], kf[0], v[0], gf[0], q[0], kb[0], v[0], gb[0])


def _outproj_kernel(ofg_ref, obg_ref, ofh_ref, obh_ref, gog_ref, hog_ref, h_ref, w_ref,
                    gng_ref, hng_ref, lng_ref, lnb_ref, o_ref, y_ref):
    def head_group(of_ref, ob_ref, gate_ref, g_ref, heads, dv, base):
        for hh in range(heads):
            sl = slice(hh * dv, (hh + 1) * dv)
            o = of_ref[:, sl].astype(F32) + ob_ref[:, sl].astype(F32)
            ms = jnp.mean(o * o, axis=-1, keepdims=True)
            y = o * lax.rsqrt(ms + RMS_EPS) * g_ref[...]
            gate = gate_ref[:, sl].astype(F32)
            y = y * (gate * (1.0 / (1.0 + jnp.exp(-gate))))
            y_ref[:, base + hh * dv: base + (hh + 1) * dv] = y.astype(BF16)

    head_group(ofg_ref, obg_ref, gog_ref, gng_ref, GLA_HEADS, GLA_DV, 0)
    head_group(ofh_ref, obh_ref, hog_ref, hng_ref, HG_HEADS, HG_DIM, GLA_WIDTH)
    m = jnp.dot(y_ref[...], w_ref[...], preferred_element_type=F32)
    o_ref[...] = _layernorm(ALPHA * h_ref[...] + m, lng_ref[...], lnb_ref[...])


def _outproj(ofg, obg, ofh, obh, z, h, w_out, gng, hng, lng, lnb):
    n = h.shape[0]
    tm = min(OUT_TM, n)
    half = lambda i: (i, 0)
    const = lambda i: (0, 0)
    return pl.pallas_call(
        _outproj_kernel,
        grid=(n // tm,),
        in_specs=[
            pl.BlockSpec((tm, GLA_WIDTH), half), pl.BlockSpec((tm, GLA_WIDTH), half),
            pl.BlockSpec((tm, HG_WIDTH), half), pl.BlockSpec((tm, HG_WIDTH), half),
            pl.BlockSpec((tm, GLA_WIDTH), lambda i: (i, 2048 // GLA_WIDTH)),
            pl.BlockSpec((tm, HG_WIDTH), lambda i: (i, 7168 // HG_WIDTH)),
            pl.BlockSpec((tm, D_MODEL), half),
            pl.BlockSpec((D_MODEL, D_MODEL), const, pipeline_mode=pl.Buffered(1)),
            pl.BlockSpec((1, GLA_DV), const), pl.BlockSpec((1, HG_DIM), const),
            pl.BlockSpec((1, D_MODEL), const), pl.BlockSpec((1, D_MODEL), const),
        ],
        out_specs=pl.BlockSpec((tm, D_MODEL), half),
        out_shape=jax.ShapeDtypeStruct((n, D_MODEL), F32),
        scratch_shapes=[pltpu.VMEM((tm, D_MODEL), BF16)],
        compiler_params=pltpu.CompilerParams(
            dimension_semantics=("arbitrary",), vmem_limit_bytes=VMEM_LIMIT),
        name="outproj_ln1",
    )(ofg, obg, ofh, obh, z, z, h, w_out, gng, hng, lng, lnb)


def _mlp_kernel(h_ref, wup_ref, wdn_ref, lng_ref, lnb_ref, o_ref, xb_ref):
    j = pl.program_id(1)

    @pl.when(j == 0)
    def _():
        xb_ref[...] = h_ref[...].astype(BF16)
        o_ref[...] = jnp.zeros_like(o_ref)

    a = jnp.dot(xb_ref[...], wup_ref[...], preferred_element_type=F32)
    a = jnp.square(jnp.maximum(a, 0.0)).astype(BF16)
    o_ref[...] += jnp.dot(a, wdn_ref[...], preferred_element_type=F32)

    @pl.when(j == pl.num_programs(1) - 1)
    def _():
        o_ref[...] = _layernorm(ALPHA * h_ref[...] + o_ref[...], lng_ref[...], lnb_ref[...])


def _mlp(h, w_up, w_down, lng, lnb):
    n = h.shape[0]
    tm = min(MLP_TM, n)
    return pl.pallas_call(
        _mlp_kernel,
        grid=(n // tm, D_FF // MLP_TF),
        in_specs=[
            pl.BlockSpec((tm, D_MODEL), lambda i, j: (i, 0)),
            pl.BlockSpec((D_MODEL, MLP_TF), lambda i, j: (0, j)),
            pl.BlockSpec((MLP_TF, D_MODEL), lambda i, j: (j, 0)),
            pl.BlockSpec((1, D_MODEL), lambda i, j: (0, 0)),
            pl.BlockSpec((1, D_MODEL), lambda i, j: (0, 0)),
        ],
        out_specs=pl.BlockSpec((tm, D_MODEL), lambda i, j: (i, 0)),
        out_shape=jax.ShapeDtypeStruct((n, D_MODEL), F32),
        scratch_shapes=[pltpu.VMEM((tm, D_MODEL), BF16)],
        compiler_params=pltpu.CompilerParams(
            dimension_semantics=("arbitrary", "arbitrary"), vmem_limit_bytes=MLP_VMEM_LIMIT),
        name="mlp_ln2",
    )(h, w_up, w_down, lng, lnb)


def _lower_bounds(p):
    c = jnp.cumsum(jax.nn.softmax(p.astype(F32), axis=0), axis=0)
    return c - c[0:1]


def _prep_layer(l, w_in, gla_w_lr2, gla_b_lr, gla_norm_g, hg_norm_g, lbs_f, lbs_b,
                w_out, ln1_g, ln1_b, w_up, w_down, ln2_g, ln2_b):
    w = w_in[l]
    glr0 = 2 * GLA_KWIDTH + 2 * GLA_WIDTH
    w_main = jnp.concatenate([w[:, :glr0], w[:, glr0 + 2 * GLA_RANK:]], axis=1).astype(BF16)
    w_glr = jnp.pad(w[:, glr0:glr0 + 2 * GLA_RANK], ((0, 0), (0, GLR_PAD - 2 * GLA_RANK))).astype(BF16)
    w2 = gla_w_lr2[l].astype(F32)
    w2c = jnp.zeros((GLR_PAD, 2 * GLA_KWIDTH), F32)
    w2c = w2c.at[:GLA_RANK, :GLA_KWIDTH].set(w2[0]).at[GLA_RANK:2 * GLA_RANK, GLA_KWIDTH:].set(w2[1])
    b2c = gla_b_lr[l].astype(F32).reshape(1, 2 * GLA_KWIDTH)
    lb = jnp.concatenate([lbs_f[l], lbs_b[l]])[None, :]
    gate_c = jnp.concatenate([lb, 1.0 - lb, jnp.log(lb), jnp.log1p(-lb), jnp.zeros((4, 2 * HG_WIDTH), F32)], axis=0)
    row = lambda a: a[l].astype(F32)[None, :]
    return dict(
        w_main=w_main, w_glr=w_glr, w2c=w2c.astype(BF16), b2c=b2c, gate_c=gate_c,
        w_out=w_out[l].astype(BF16), gng=row(gla_norm_g), hng=row(hg_norm_g),
        ln1_g=row(ln1_g), ln1_b=row(ln1_b),
        w_up=w_up[l].astype(BF16), w_down=w_down[l].astype(BF16),
        ln2_g=row(ln2_g), ln2_b=row(ln2_b),
    )


def _trunk(x, layers):
    batch, seq, _ = x.shape
    h = x.reshape(batch * seq, D_MODEL).astype(F32)
    for p in layers:
        z, logf, ga = _inproj(h, p["w_main"], p["w_glr"], p["w2c"], p["b2c"], p["gate_c"])
        ofg, obg = _scan((z, 0), (z, 1), (z, 1), (z, 1), (ga, 0), (ga, 1),
                         batch=batch, seq=seq, heads=GLA_HEADS, dk=GLA_DK, dv=GLA_DV, chunk=GLA_CHUNK)
        ofh, obh = _scan((z, 3), (z, 4), (z, 5), (z, 6), (logf, 0), (logf, 1),
                         batch=batch, seq=seq, heads=HG_HEADS, dk=HG_DIM, dv=HG_DIM, chunk=HG_CHUNK)
        h = _outproj(ofg, obg, ofh, obh, z, h, p["w_out"], p["gng"], p["hng"], p["ln1_g"], p["ln1_b"])
        h = _mlp(h, p["w_up"], p["w_down"], p["ln2_g"], p["ln2_b"])
    return h.reshape(batch, seq, D_MODEL).astype(x.dtype)


def kernel(x_prompt, x_sample, w_in, gla_w_lr2, gla_b_lr, gla_norm_g, hg_norm_g, lower_bounds,
           w_out, ln1_g, ln1_b, w_up, w_down, ln2_g, ln2_b):
    lbs_f = _lower_bounds(lower_bounds[0])
    lbs_b = _lower_bounds(lower_bounds[1])
    layers = [_prep_layer(l, w_in, gla_w_lr2, gla_b_lr, gla_norm_g, hg_norm_g, lbs_f, lbs_b,
                          w_out, ln1_g, ln1_b, w_up, w_down, ln2_g, ln2_b) for l in range(DEPTH)]
    return (_trunk(x_prompt, layers), _trunk(x_sample, layers))
```

```python
import functools

import jax
import jax.numpy as jnp
from jax import lax
from jax.experimental import pallas as pl
from jax.experimental.pallas import tpu as pltpu

F32 = jnp.float32
BF16 = jnp.bfloat16

D_MODEL = 2048
DEPTH = 2
GLA_HEADS = 4
GLA_WIDTH = D_MODEL // 2
GLA_DV = GLA_WIDTH // GLA_HEADS
GLA_DK = GLA_DV // 2
GLA_KWIDTH = GLA_HEADS * GLA_DK
GLA_RANK = 16
GLA_TAU = 16.0
HG_DIM = 128
HG_WIDTH = D_MODEL - GLA_WIDTH
HG_HEADS = HG_WIDTH // HG_DIM
D_FF = 4 * D_MODEL
LN_EPS = 1e-5
RMS_EPS = 1e-6
F32_TINY = 1.1754943508222875e-38
ALPHA = (2.0 * DEPTH) ** 0.25

LANES = 128
GLR_PAD = LANES
Z_WIDTH = 8192
VMEM_LIMIT = 48 * 1024 * 1024

IN_TN = 512
IN_TM = 1024
MXU_COLS = 256
GATE_TILE_LO = 4096 // IN_TN
GATE_TILE_HI = 6144 // IN_TN

GLA_CHUNK = 256
HG_CHUNK = 128
MAX_HALF_RANGE = 80.0

OUT_TM = 512
MLP_TM = 1024
MLP_TF = 512
MLP_VMEM_LIMIT = 58 * 1024 * 1024

_NT = (((1,), (1,)), ((), ()))
_TN = (((0,), (0,)), ((), ()))


def _log_sigmoid(x):
    return jnp.minimum(x, 0.0) - jnp.log(1.0 + jnp.exp(-jnp.abs(x)))


def _layernorm(r, g, b):
    mu = jnp.mean(r, axis=-1, keepdims=True)
    d = r - mu
    var = jnp.mean(d * d, axis=-1, keepdims=True)
    return d * lax.rsqrt(var + LN_EPS) * g + b


def _inproj_kernel(x_ref, w_ref, wglr_ref, w2_ref, b2_ref, gc_ref, z_ref, logf_ref, ga_ref, xb_ref):
    j = pl.program_id(1)
    is_gate = jnp.logical_and(j >= GATE_TILE_LO, j < GATE_TILE_HI)
    col_groups = lambda width: [slice(c, c + MXU_COLS) for c in range(0, width, MXU_COLS)]

    @pl.when(j == 0)
    def _():
        xb = x_ref[...].astype(BF16)
        xb_ref[...] = xb
        glr = jnp.dot(xb, wglr_ref[...], preferred_element_type=F32).astype(BF16)
        for cols in col_groups(ga_ref.shape[1]):
            a = jnp.dot(glr, w2_ref[:, cols], preferred_element_type=F32) + b2_ref[:, cols]
            ga_ref[:, cols] = _log_sigmoid(a) * (1.0 / GLA_TAU)
        for cols in col_groups(z_ref.shape[1]):
            q = jnp.dot(xb, w_ref[:, cols], preferred_element_type=F32) * (GLA_DK ** -0.5)
            z_ref[:, cols] = q.astype(BF16)

    @pl.when(jnp.logical_and(j > 0, jnp.logical_not(is_gate)))
    def _():
        z_ref[...] = jnp.dot(xb_ref[...], w_ref[...], preferred_element_type=F32).astype(BF16)

    @pl.when(is_gate)
    def _():
        f_min = None
        half = x_ref.shape[0] // 2
        for cols in col_groups(z_ref.shape[1]):
            lb = gc_ref[0:1, cols]
            one_minus_lb = gc_ref[1:2, cols]
            for rows in (slice(0, half), slice(half, 2 * half)):
                acc = jnp.dot(xb_ref[rows, :], w_ref[:, cols], preferred_element_type=F32)
                e = jnp.exp(-jnp.abs(acc))
                r = 1.0 / (1.0 + e)
                er = e * r
                pos = acc >= 0.0
                f = lb + one_minus_lb * jnp.where(pos, r, er)
                logf_ref[rows, cols] = jnp.log(f)
                z_ref[rows, cols] = (one_minus_lb * jnp.where(pos, er, r)).astype(BF16)
                f_min = f if f_min is None else jnp.minimum(f_min, f)

        @pl.when(jnp.logical_not(jnp.min(f_min) >= F32_TINY))
        def _():
            acc = jnp.dot(xb_ref[...], w_ref[...], preferred_element_type=F32)
            log_lb = gc_ref[2:3, :]
            x2 = gc_ref[3:4, :] + _log_sigmoid(acc)
            logf_ref[...] = (jnp.maximum(log_lb, x2)
                             + jnp.log(1.0 + jnp.exp(-jnp.abs(log_lb - x2))))


def _inproj(x, layer, w_main, w_glr, w2c, b2c, gate_c):
    n = x.shape[0]
    tm = min(IN_TM, n)
    n_gate = GATE_TILE_HI - GATE_TILE_LO
    gate_idx = lambda i, j: jnp.clip(j - GATE_TILE_LO, 0, n_gate - 1)
    return pl.pallas_call(
        _inproj_kernel,
        grid=(n // tm, Z_WIDTH // IN_TN),
        in_specs=[
            pl.BlockSpec((tm, D_MODEL), lambda i, j: (i, 0)),
            pl.BlockSpec((None, D_MODEL, IN_TN), lambda i, j: (layer, 0, j)),
            pl.BlockSpec((None, D_MODEL, GLR_PAD), lambda i, j: (layer, 0, 0)),
            pl.BlockSpec((GLR_PAD, 2 * GLA_KWIDTH), lambda i, j: (0, 0)),
            pl.BlockSpec((1, 2 * GLA_KWIDTH), lambda i, j: (0, 0)),
            pl.BlockSpec((8, IN_TN), lambda i, j: (0, gate_idx(i, j))),
        ],
        out_specs=[
            pl.BlockSpec((tm, IN_TN), lambda i, j: (i, j)),
            pl.BlockSpec((tm, IN_TN), lambda i, j: (i, gate_idx(i, j))),
            pl.BlockSpec((tm, 2 * GLA_KWIDTH), lambda i, j: (i, 0)),
        ],
        out_shape=[
            jax.ShapeDtypeStruct((n, Z_WIDTH), BF16),
            jax.ShapeDtypeStruct((n, 2 * HG_WIDTH), F32),
            jax.ShapeDtypeStruct((n, 2 * GLA_KWIDTH), F32),
        ],
        scratch_shapes=[pltpu.VMEM((tm, D_MODEL), BF16)],
        compiler_params=pltpu.CompilerParams(
            dimension_semantics=("arbitrary", "arbitrary"), vmem_limit_bytes=VMEM_LIMIT),
        name="inproj",
    )(x, w_main, w_glr, w2c, b2c, gate_c)


def _cumsum_rows(tri, g):
    hi = g.astype(BF16)
    lo = (g - hi.astype(F32)).astype(BF16)
    return jnp.dot(jnp.concatenate([tri, tri], axis=1), jnp.concatenate([hi, lo], axis=0),
                   preferred_element_type=F32)


def _dir_fast(q, k, v, b, tot, st, mask):
    m = 0.5 * tot
    qt = q * jnp.exp(b - m).astype(BF16)
    kt = k * jnp.exp(m - b).astype(BF16)
    c = q.shape[0]
    ssc = (st * jnp.exp(m)).astype(BF16)
    r = lax.dot_general(qt, jnp.concatenate([kt, ssc], axis=0), _NT, preferred_element_type=F32)
    s = jnp.where(mask, r[:, :c], 0.0).astype(BF16)
    o = jnp.dot(s, v, preferred_element_type=F32) + r[:, c:]
    u = lax.dot_general(v, kt, _TN, preferred_element_type=F32)
    return o, st * jnp.exp(tot) + u * jnp.exp(tot - m)


def _dir_slow(q, k, v, b, tot, st, reverse, row_ref):
    c = q.shape[0]
    q = q.astype(F32)
    k = k.astype(F32)
    row_ref[0] = b
    row_ref[1] = q
    jj = lax.broadcasted_iota(jnp.int32, (c, 1), 0)
    lane = lax.broadcasted_iota(jnp.int32, (c, c), 1)

    def body(i, s_t):
        bi = row_ref[0, pl.ds(i, 1), :]
        qi = row_ref[1, pl.ds(i, 1), :]
        valid = (jj >= i) if reverse else (jj <= i)
        w = jnp.where(valid, jnp.exp(jnp.minimum(bi - b, 0.0)), 0.0)
        col = jnp.sum(qi * k * w, axis=-1, keepdims=True)
        return jnp.where(lane == i, col, s_t)

    s_t = lax.fori_loop(0, c, body, jnp.zeros((c, c), F32))
    o = lax.dot_general(s_t.astype(BF16), v, _TN, preferred_element_type=F32)
    qin = (q * jnp.exp(b)).astype(BF16)
    o = o + lax.dot_general(qin, st.astype(BF16), _NT, preferred_element_type=F32)
    kout = (k * jnp.exp(tot - b)).astype(BF16)
    u = lax.dot_general(v, kout, _TN, preferred_element_type=F32)
    return o, st * jnp.exp(tot) + u


def _scan_kernel(qf_ref, kf_ref, vf_ref, gf_ref, qb_ref, kb_ref, vb_ref, gb_ref,
                 of_ref, ob_ref, sf_ref, sb_ref, row_ref, *, heads, dk, dv):
    c = qf_ref.shape[0]

    @pl.when(pl.program_id(1) == 0)
    def _():
        sf_ref[...] = jnp.zeros_like(sf_ref)
        sb_ref[...] = jnp.zeros_like(sb_ref)

    row = lax.broadcasted_iota(jnp.int32, (c, c), 0)
    col = lax.broadcasted_iota(jnp.int32, (c, c), 1)
    lower = row >= col
    upper = row <= col
    bf_all = _cumsum_rows(jnp.where(lower, 1.0, 0.0).astype(BF16), gf_ref[...])
    bb_all = _cumsum_rows(jnp.where(upper, 1.0, 0.0).astype(BF16), gb_ref[...])
    tot_f = bf_all[c - 1:c, :]
    tot_b = bb_all[0:1, :]
    span = jnp.maximum(jnp.max(-tot_f), jnp.max(-tot_b))
    fast = span < 2.0 * MAX_HALF_RANGE

    def run(step_f, step_b):
        for h in range(heads):
            ks = slice(h * dk, (h + 1) * dk)
            vs = slice(h * dv, (h + 1) * dv)
            o, s_new = step_f(qf_ref[:, ks], kf_ref[:, ks], vf_ref[:, vs], bf_all[:, ks], tot_f[:, ks], sf_ref[h])
            of_ref[:, vs] = o.astype(BF16)
            sf_ref[h] = s_new
            o, s_new = step_b(qb_ref[:, ks], kb_ref[:, ks], vb_ref[:, vs], bb_all[:, ks], tot_b[:, ks], sb_ref[h])
            ob_ref[:, vs] = o.astype(BF16)
            sb_ref[h] = s_new

    @pl.when(fast)
    def _():
        run(functools.partial(_dir_fast, mask=lower), functools.partial(_dir_fast, mask=upper))

    @pl.when(jnp.logical_not(fast))
    def _():
        run(functools.partial(_dir_slow, reverse=False, row_ref=row_ref),
            functools.partial(_dir_slow, reverse=True, row_ref=row_ref))


def _scan(q, kf, kb, v, gf, gb, *, batch, seq, heads, dk, dv, chunk):
    n = batch * seq
    c = min(chunk, seq)
    nc = seq // c
    kw, vw = heads * dk, heads * dv
    fwd = lambda blk: (lambda b, i: (b * nc + i, blk))
    bwd = lambda blk: (lambda b, i: (b * nc + (nc - 1 - i), blk))
    out_f = lambda b, i: (b * nc + i, 0)
    out_b = lambda b, i: (b * nc + (nc - 1 - i), 0)
    return pl.pallas_call(
        functools.partial(_scan_kernel, heads=heads, dk=dk, dv=dv),
        grid=(batch, nc),
        in_specs=[
            pl.BlockSpec((c, kw), fwd(q[1])), pl.BlockSpec((c, kw), fwd(kf[1])),
            pl.BlockSpec((c, vw), fwd(v[1])), pl.BlockSpec((c, kw), fwd(gf[1])),
            pl.BlockSpec((c, kw), bwd(q[1])), pl.BlockSpec((c, kw), bwd(kb[1])),
            pl.BlockSpec((c, vw), bwd(v[1])), pl.BlockSpec((c, kw), bwd(gb[1])),
        ],
        out_specs=[pl.BlockSpec((c, vw), out_f), pl.BlockSpec((c, vw), out_b)],
        out_shape=[jax.ShapeDtypeStruct((n, vw), BF16), jax.ShapeDtypeStruct((n, vw), BF16)],
        scratch_shapes=[
            pltpu.VMEM((heads, dv, dk), F32),
            pltpu.VMEM((heads, dv, dk), F32),
            pltpu.VMEM((2, c, dk), F32),
        ],
        compiler_params=pltpu.CompilerParams(
            dimension_semantics=("arbitrary", "arbitrary"), vmem_limit_bytes=VMEM_LIMIT),
        name=f"scan_h{heads}",
    )(q[0], kf[0], v[0], gf[0], q[0], kb[0], v[0], gb[0])


def _outproj_kernel(ofg_ref, obg_ref, ofh_ref, obh_ref, gog_ref, hog_ref, h_ref, w_ref,
                    gng_ref, hng_ref, lng_ref, lnb_ref, o_ref, y_ref):
    def head_group(of_ref, ob_ref, gate_ref, g_ref, heads, dv, base):
        for hh in range(heads):
            sl = slice(hh * dv, (hh + 1) * dv)
            o = of_ref[:, sl].astype(F32) + ob_ref[:, sl].astype(F32)
            ms = jnp.mean(o * o, axis=-1, keepdims=True)
            y = o * lax.rsqrt(ms + RMS_EPS) * g_ref[...]
            gate = gate_ref[:, sl].astype(F32)
            y = y * (gate * (1.0 / (1.0 + jnp.exp(-gate))))
            y_ref[:, base + hh * dv: base + (hh + 1) * dv] = y.astype(BF16)

    head_group(ofg_ref, obg_ref, gog_ref, gng_ref, GLA_HEADS, GLA_DV, 0)
    head_group(ofh_ref, obh_ref, hog_ref, hng_ref, HG_HEADS, HG_DIM, GLA_WIDTH)
    m = jnp.dot(y_ref[...], w_ref[...], preferred_element_type=F32)
    o_ref[...] = _layernorm(ALPHA * h_ref[...] + m, lng_ref[...], lnb_ref[...])


def _outproj(ofg, obg, ofh, obh, z, h, layer, w_out, gng, hng, lng, lnb):
    n = h.shape[0]
    tm = min(OUT_TM, n)
    half = lambda i: (i, 0)
    const = lambda i: (0, 0)
    return pl.pallas_call(
        _outproj_kernel,
        grid=(n // tm,),
        in_specs=[
            pl.BlockSpec((tm, GLA_WIDTH), half), pl.BlockSpec((tm, GLA_WIDTH), half),
            pl.BlockSpec((tm, HG_WIDTH), half), pl.BlockSpec((tm, HG_WIDTH), half),
            pl.BlockSpec((tm, GLA_WIDTH), lambda i: (i, 2048 // GLA_WIDTH)),
            pl.BlockSpec((tm, HG_WIDTH), lambda i: (i, 7168 // HG_WIDTH)),
            pl.BlockSpec((tm, D_MODEL), half),
            pl.BlockSpec((None, D_MODEL, D_MODEL), lambda i: (layer, 0, 0), pipeline_mode=pl.Buffered(1)),
            pl.BlockSpec((1, GLA_DV), const), pl.BlockSpec((1, HG_DIM), const),
            pl.BlockSpec((1, D_MODEL), const), pl.BlockSpec((1, D_MODEL), const),
        ],
        out_specs=pl.BlockSpec((tm, D_MODEL), half),
        out_shape=jax.ShapeDtypeStruct((n, D_MODEL), F32),
        scratch_shapes=[pltpu.VMEM((tm, D_MODEL), BF16)],
        compiler_params=pltpu.CompilerParams(
            dimension_semantics=("arbitrary",), vmem_limit_bytes=VMEM_LIMIT),
        name="outproj_ln1",
    )(ofg, obg, ofh, obh, z, z, h, w_out, gng, hng, lng, lnb)


def _mlp_kernel(h_ref, wup_ref, wdn_ref, lng_ref, lnb_ref, o_ref, xb_ref):
    j = pl.program_id(1)

    @pl.when(j == 0)
    def _():
        xb_ref[...] = h_ref[...].astype(BF16)
        o_ref[...] = jnp.zeros_like(o_ref)

    a = jnp.dot(xb_ref[...], wup_ref[...], preferred_element_type=F32)
    a = jnp.square(jnp.maximum(a, 0.0)).astype(BF16)
    o_ref[...] += jnp.dot(a, wdn_ref[...], preferred_element_type=F32)

    @pl.when(j == pl.num_programs(1) - 1)
    def _():
        o_ref[...] = _layernorm(ALPHA * h_ref[...] + o_ref[...], lng_ref[...], lnb_ref[...])


def _mlp(h, layer, w_up, w_down, lng, lnb):
    n = h.shape[0]
    tm = min(MLP_TM, n)
    return pl.pallas_call(
        _mlp_kernel,
        grid=(n // tm, D_FF // MLP_TF),
        in_specs=[
            pl.BlockSpec((tm, D_MODEL), lambda i, j: (i, 0)),
            pl.BlockSpec((None, D_MODEL, MLP_TF), lambda i, j: (layer, 0, j)),
            pl.BlockSpec((None, MLP_TF, D_MODEL), lambda i, j: (layer, j, 0)),
            pl.BlockSpec((1, D_MODEL), lambda i, j: (0, 0)),
            pl.BlockSpec((1, D_MODEL), lambda i, j: (0, 0)),
        ],
        out_specs=pl.BlockSpec((tm, D_MODEL), lambda i, j: (i, 0)),
        out_shape=jax.ShapeDtypeStruct((n, D_MODEL), F32),
        scratch_shapes=[pltpu.VMEM((tm, D_MODEL), BF16)],
        compiler_params=pltpu.CompilerParams(
            dimension_semantics=("arbitrary", "arbitrary"), vmem_limit_bytes=MLP_VMEM_LIMIT),
        name="mlp_ln2",
    )(h, w_up, w_down, lng, lnb)


def _lower_bounds(p):
    c = jnp.cumsum(jax.nn.softmax(p.astype(F32), axis=0), axis=0)
    return c - c[0:1]


def _prep_params(w_in, gla_w_lr2, gla_b_lr, gla_norm_g, hg_norm_g, lbs_f, lbs_b,
                 w_out, ln1_g, ln1_b, w_up, w_down, ln2_g, ln2_b):
    glr0 = 2 * GLA_KWIDTH + 2 * GLA_WIDTH
    w_main = jnp.concatenate([w_in[..., :glr0], w_in[..., glr0 + 2 * GLA_RANK:]], axis=-1).astype(BF16)
    w_glr = jnp.pad(w_in[..., glr0:glr0 + 2 * GLA_RANK],
                    ((0, 0), (0, 0), (0, GLR_PAD - 2 * GLA_RANK))).astype(BF16)
    stacked = dict(w_main=w_main, w_glr=w_glr, w_out=w_out.astype(BF16),
                   w_up=w_up.astype(BF16), w_down=w_down.astype(BF16))
    layers = []
    for l in range(DEPTH):
        w2 = gla_w_lr2[l].astype(F32)
        w2c = jnp.zeros((GLR_PAD, 2 * GLA_KWIDTH), F32)
        w2c = w2c.at[:GLA_RANK, :GLA_KWIDTH].set(w2[0]).at[GLA_RANK:2 * GLA_RANK, GLA_KWIDTH:].set(w2[1])
        lb = jnp.concatenate([lbs_f[l], lbs_b[l]])[None, :]
        row = lambda a: a[l].astype(F32)[None, :]
        layers.append(dict(
            w2c=w2c.astype(BF16), b2c=gla_b_lr[l].astype(F32).reshape(1, 2 * GLA_KWIDTH),
            gate_c=jnp.concatenate([lb, 1.0 - lb, jnp.log(lb), jnp.log1p(-lb),
                                    jnp.zeros((4, 2 * HG_WIDTH), F32)], axis=0),
            gng=row(gla_norm_g), hng=row(hg_norm_g), ln1_g=row(ln1_g), ln1_b=row(ln1_b),
            ln2_g=row(ln2_g), ln2_b=row(ln2_b)))
    return stacked, layers


def _trunk(x, stacked, layers):
    batch, seq, _ = x.shape
    h = x.reshape(batch * seq, D_MODEL).astype(F32)
    for l, p in enumerate(layers):
        z, logf, ga = _inproj(h, l, stacked["w_main"], stacked["w_glr"], p["w2c"], p["b2c"], p["gate_c"])
        ofg, obg = _scan((z, 0), (z, 1), (z, 1), (z, 1), (ga, 0), (ga, 1),
                         batch=batch, seq=seq, heads=GLA_HEADS, dk=GLA_DK, dv=GLA_DV, chunk=GLA_CHUNK)
        ofh, obh = _scan((z, 3), (z, 4), (z, 5), (z, 6), (logf, 0), (logf, 1),
                         batch=batch, seq=seq, heads=HG_HEADS, dk=HG_DIM, dv=HG_DIM, chunk=HG_CHUNK)
        h = _outproj(ofg, obg, ofh, obh, z, h, l, stacked["w_out"], p["gng"], p["hng"], p["ln1_g"], p["ln1_b"])
        h = _mlp(h, l, stacked["w_up"], stacked["w_down"], p["ln2_g"], p["ln2_b"])
    return h.reshape(batch, seq, D_MODEL).astype(x.dtype)


def kernel(x_prompt, x_sample, w_in, gla_w_lr2, gla_b_lr, gla_norm_g, hg_norm_g, lower_bounds,
           w_out, ln1_g, ln1_b, w_up, w_down, ln2_g, ln2_b):
    lbs_f = _lower_bounds(lower_bounds[0])
    lbs_b = _lower_bounds(lower_bounds[1])
    stacked, layers = _prep_params(w_in, gla_w_lr2, gla_b_lr, gla_norm_g, hg_norm_g, lbs_f, lbs_b,
                                   w_out, ln1_g, ln1_b, w_up, w_down, ln2_g, ln2_b)
    return (_trunk(x_prompt, stacked, layers), _trunk(x_sample, stacked, layers))
```

```python
import functools

import jax
import jax.numpy as jnp
from jax import lax
from jax.experimental import pallas as pl
from jax.experimental.pallas import tpu as pltpu

F32 = jnp.float32
BF16 = jnp.bfloat16

D_MODEL = 2048
DEPTH = 2
GLA_HEADS = 4
GLA_WIDTH = D_MODEL // 2
GLA_DV = GLA_WIDTH // GLA_HEADS
GLA_DK = GLA_DV // 2
GLA_KWIDTH = GLA_HEADS * GLA_DK
GLA_RANK = 16
GLA_TAU = 16.0
HG_DIM = 128
HG_WIDTH = D_MODEL - GLA_WIDTH
HG_HEADS = HG_WIDTH // HG_DIM
D_FF = 4 * D_MODEL
LN_EPS = 1e-5
RMS_EPS = 1e-6
F32_TINY = 1.1754943508222875e-38
ALPHA = (2.0 * DEPTH) ** 0.25

LANES = 128
GLR_PAD = LANES
Z_WIDTH = 8192
VMEM_LIMIT = 48 * 1024 * 1024
IN_VMEM_LIMIT = 58 * 1024 * 1024

IN_TN = 1024
IN_TM = 1024
MXU_COLS = 256
GATE_TILE_LO = 4096 // IN_TN
GATE_TILE_HI = 6144 // IN_TN

GLA_CHUNK = 256
HG_CHUNK = 128
MAX_HALF_RANGE = 80.0

OUT_TM = 512
MLP_TM = 1024
MLP_TF = 512
MLP_VMEM_LIMIT = 58 * 1024 * 1024

_NT = (((1,), (1,)), ((), ()))
_TN = (((0,), (0,)), ((), ()))


def _log_sigmoid(x):
    return jnp.minimum(x, 0.0) - jnp.log(1.0 + jnp.exp(-jnp.abs(x)))


def _layernorm(r, g, b):
    mu = jnp.mean(r, axis=-1, keepdims=True)
    d = r - mu
    var = jnp.mean(d * d, axis=-1, keepdims=True)
    return d * lax.rsqrt(var + LN_EPS) * g + b


def _inproj_kernel(x_ref, w_ref, wglr_ref, w2_ref, b2_ref, gc_ref, z_ref, logf_ref, ga_ref, xb_ref):
    j = pl.program_id(1)
    is_gate = jnp.logical_and(j >= GATE_TILE_LO, j < GATE_TILE_HI)
    col_groups = lambda width: [slice(c, c + MXU_COLS) for c in range(0, width, MXU_COLS)]

    last = pl.num_programs(1) - 1

    def project_plain():
        for cols in col_groups(z_ref.shape[1]):
            z_ref[:, cols] = jnp.dot(xb_ref[...], w_ref[:, cols], preferred_element_type=F32).astype(BF16)

    @pl.when(j == 0)
    def _():
        xb = x_ref[...].astype(BF16)
        xb_ref[...] = xb
        for cols in col_groups(z_ref.shape[1]):
            zc = jnp.dot(xb, w_ref[:, cols], preferred_element_type=F32)
            if cols.stop <= GLA_KWIDTH:
                zc = zc * (GLA_DK ** -0.5)
            z_ref[:, cols] = zc.astype(BF16)

    @pl.when(jnp.logical_and(jnp.logical_and(j > 0, j < last), jnp.logical_not(is_gate)))
    def _():
        project_plain()

    @pl.when(j == last)
    def _():
        project_plain()
        glr = jnp.dot(xb_ref[...], wglr_ref[...], preferred_element_type=F32).astype(BF16)
        for cols in col_groups(ga_ref.shape[1]):
            a = jnp.dot(glr, w2_ref[:, cols], preferred_element_type=F32) + b2_ref[:, cols]
            ga_ref[:, cols] = _log_sigmoid(a) * (1.0 / GLA_TAU)

    @pl.when(is_gate)
    def _():
        f_min = None
        half = x_ref.shape[0] // 2
        for cols in col_groups(z_ref.shape[1]):
            lb = gc_ref[0:1, cols]
            one_minus_lb = gc_ref[1:2, cols]
            for rows in (slice(0, half), slice(half, 2 * half)):
                acc = jnp.dot(xb_ref[rows, :], w_ref[:, cols], preferred_element_type=F32)
                e = jnp.exp(-jnp.abs(acc))
                r = 1.0 / (1.0 + e)
                er = e * r
                pos = acc >= 0.0
                f = lb + one_minus_lb * jnp.where(pos, r, er)
                logf_ref[rows, cols] = jnp.log(f)
                z_ref[rows, cols] = (one_minus_lb * jnp.where(pos, er, r)).astype(BF16)
                f_min = f if f_min is None else jnp.minimum(f_min, f)

        @pl.when(jnp.logical_not(jnp.min(f_min) >= F32_TINY))
        def _():
            acc = jnp.dot(xb_ref[...], w_ref[...], preferred_element_type=F32)
            log_lb = gc_ref[2:3, :]
            x2 = gc_ref[3:4, :] + _log_sigmoid(acc)
            logf_ref[...] = (jnp.maximum(log_lb, x2)
                             + jnp.log(1.0 + jnp.exp(-jnp.abs(log_lb - x2))))


def _inproj(x, layer, w_main, w_glr, w2c, b2c, gate_c):
    n = x.shape[0]
    tm = min(IN_TM, n)
    n_gate = GATE_TILE_HI - GATE_TILE_LO
    gate_idx = lambda i, j: jnp.clip(j - GATE_TILE_LO, 0, n_gate - 1)
    return pl.pallas_call(
        _inproj_kernel,
        grid=(n // tm, Z_WIDTH // IN_TN),
        in_specs=[
            pl.BlockSpec((tm, D_MODEL), lambda i, j: (i, 0)),
            pl.BlockSpec((None, D_MODEL, IN_TN), lambda i, j: (layer, 0, j)),
            pl.BlockSpec((None, D_MODEL, GLR_PAD), lambda i, j: (layer, 0, 0)),
            pl.BlockSpec((GLR_PAD, 2 * GLA_KWIDTH), lambda i, j: (0, 0)),
            pl.BlockSpec((1, 2 * GLA_KWIDTH), lambda i, j: (0, 0)),
            pl.BlockSpec((8, IN_TN), lambda i, j: (0, gate_idx(i, j))),
        ],
        out_specs=[
            pl.BlockSpec((tm, IN_TN), lambda i, j: (i, j)),
            pl.BlockSpec((tm, IN_TN), lambda i, j: (i, gate_idx(i, j))),
            pl.BlockSpec((tm, 2 * GLA_KWIDTH), lambda i, j: (i, 0)),
        ],
        out_shape=[
            jax.ShapeDtypeStruct((n, Z_WIDTH), BF16),
            jax.ShapeDtypeStruct((n, 2 * HG_WIDTH), F32),
            jax.ShapeDtypeStruct((n, 2 * GLA_KWIDTH), F32),
        ],
        scratch_shapes=[pltpu.VMEM((tm, D_MODEL), BF16)],
        compiler_params=pltpu.CompilerParams(
            dimension_semantics=("arbitrary", "arbitrary"), vmem_limit_bytes=IN_VMEM_LIMIT),
        name="inproj",
    )(x, w_main, w_glr, w2c, b2c, gate_c)


def _cumsum_rows(tri, g):
    hi = g.astype(BF16)
    lo = (g - hi.astype(F32)).astype(BF16)
    return jnp.dot(jnp.concatenate([tri, tri], axis=1), jnp.concatenate([hi, lo], axis=0),
                   preferred_element_type=F32)


def _dir_fast(q, k, v, b, tot, st, mask):
    m = 0.5 * tot
    qt = q * jnp.exp(b - m).astype(BF16)
    kt = k * jnp.exp(m - b).astype(BF16)
    c = q.shape[0]
    ssc = (st * jnp.exp(m)).astype(BF16)
    r = lax.dot_general(qt, jnp.concatenate([kt, ssc], axis=0), _NT, preferred_element_type=F32)
    s = jnp.where(mask, r[:, :c], 0.0).astype(BF16)
    o = jnp.dot(s, v, preferred_element_type=F32) + r[:, c:]
    u = lax.dot_general(v, kt, _TN, preferred_element_type=F32)
    return o, st * jnp.exp(tot) + u * jnp.exp(tot - m)


def _dir_slow(q, k, v, b, tot, st, reverse, row_ref):
    c = q.shape[0]
    q = q.astype(F32)
    k = k.astype(F32)
    row_ref[0] = b
    row_ref[1] = q
    jj = lax.broadcasted_iota(jnp.int32, (c, 1), 0)
    lane = lax.broadcasted_iota(jnp.int32, (c, c), 1)

    def body(i, s_t):
        bi = row_ref[0, pl.ds(i, 1), :]
        qi = row_ref[1, pl.ds(i, 1), :]
        valid = (jj >= i) if reverse else (jj <= i)
        w = jnp.where(valid, jnp.exp(jnp.minimum(bi - b, 0.0)), 0.0)
        col = jnp.sum(qi * k * w, axis=-1, keepdims=True)
        return jnp.where(lane == i, col, s_t)

    s_t = lax.fori_loop(0, c, body, jnp.zeros((c, c), F32))
    o = lax.dot_general(s_t.astype(BF16), v, _TN, preferred_element_type=F32)
    qin = (q * jnp.exp(b)).astype(BF16)
    o = o + lax.dot_general(qin, st.astype(BF16), _NT, preferred_element_type=F32)
    kout = (k * jnp.exp(tot - b)).astype(BF16)
    u = lax.dot_general(v, kout, _TN, preferred_element_type=F32)
    return o, st * jnp.exp(tot) + u


def _scan_kernel(qf_ref, kf_ref, vf_ref, gf_ref, qb_ref, kb_ref, vb_ref, gb_ref,
                 of_ref, ob_ref, sf_ref, sb_ref, row_ref, *, heads, dk, dv):
    c = qf_ref.shape[0]

    @pl.when(pl.program_id(1) == 0)
    def _():
        sf_ref[...] = jnp.zeros_like(sf_ref)
        sb_ref[...] = jnp.zeros_like(sb_ref)

    row = lax.broadcasted_iota(jnp.int32, (c, c), 0)
    col = lax.broadcasted_iota(jnp.int32, (c, c), 1)
    lower = row >= col
    upper = row <= col
    bf_all = _cumsum_rows(jnp.where(lower, 1.0, 0.0).astype(BF16), gf_ref[...])
    bb_all = _cumsum_rows(jnp.where(upper, 1.0, 0.0).astype(BF16), gb_ref[...])
    tot_f = bf_all[c - 1:c, :]
    tot_b = bb_all[0:1, :]
    span = jnp.maximum(jnp.max(-tot_f), jnp.max(-tot_b))
    fast = span < 2.0 * MAX_HALF_RANGE

    def run(step_f, step_b):
        for h in range(heads):
            ks = slice(h * dk, (h + 1) * dk)
            vs = slice(h * dv, (h + 1) * dv)
            o, s_new = step_f(qf_ref[:, ks], kf_ref[:, ks], vf_ref[:, vs], bf_all[:, ks], tot_f[:, ks], sf_ref[h])
            of_ref[:, vs] = o.astype(BF16)
            sf_ref[h] = s_new
            o, s_new = step_b(qb_ref[:, ks], kb_ref[:, ks], vb_ref[:, vs], bb_all[:, ks], tot_b[:, ks], sb_ref[h])
            ob_ref[:, vs] = o.astype(BF16)
            sb_ref[h] = s_new

    @pl.when(fast)
    def _():
        run(functools.partial(_dir_fast, mask=lower), functools.partial(_dir_fast, mask=upper))

    @pl.when(jnp.logical_not(fast))
    def _():
        run(functools.partial(_dir_slow, reverse=False, row_ref=row_ref),
            functools.partial(_dir_slow, reverse=True, row_ref=row_ref))


def _scan(q, kf, kb, v, gf, gb, *, batch, seq, heads, dk, dv, chunk):
    n = batch * seq
    c = min(chunk, seq)
    nc = seq // c
    kw, vw = heads * dk, heads * dv
    fwd = lambda blk: (lambda b, i: (b * nc + i, blk))
    bwd = lambda blk: (lambda b, i: (b * nc + (nc - 1 - i), blk))
    out_f = lambda b, i: (b * nc + i, 0)
    out_b = lambda b, i: (b * nc + (nc - 1 - i), 0)
    return pl.pallas_call(
        functools.partial(_scan_kernel, heads=heads, dk=dk, dv=dv),
        grid=(batch, nc),
        in_specs=[
            pl.BlockSpec((c, kw), fwd(q[1])), pl.BlockSpec((c, kw), fwd(kf[1])),
            pl.BlockSpec((c, vw), fwd(v[1])), pl.BlockSpec((c, kw), fwd(gf[1])),
            pl.BlockSpec((c, kw), bwd(q[1])), pl.BlockSpec((c, kw), bwd(kb[1])),
            pl.BlockSpec((c, vw), bwd(v[1])), pl.BlockSpec((c, kw), bwd(gb[1])),
        ],
        out_specs=[pl.BlockSpec((c, vw), out_f), pl.BlockSpec((c, vw), out_b)],
        out_shape=[jax.ShapeDtypeStruct((n, vw), BF16), jax.ShapeDtypeStruct((n, vw), BF16)],
        scratch_shapes=[
            pltpu.VMEM((heads, dv, dk), F32),
            pltpu.VMEM((heads, dv, dk), F32),
            pltpu.VMEM((2, c, dk), F32),
        ],
        compiler_params=pltpu.CompilerParams(
            dimension_semantics=("arbitrary", "arbitrary"), vmem_limit_bytes=VMEM_LIMIT),
        name=f"scan_h{heads}",
    )(q[0], kf[0], v[0], gf[0], q[0], kb[0], v[0], gb[0])


def _outproj_kernel(ofg_ref, obg_ref, ofh_ref, obh_ref, gog_ref, hog_ref, h_ref, w_ref,
                    gng_ref, hng_ref, lng_ref, lnb_ref, o_ref, y_ref):
    def head_group(of_ref, ob_ref, gate_ref, g_ref, heads, dv, base):
        for hh in range(heads):
            sl = slice(hh * dv, (hh + 1) * dv)
            o = of_ref[:, sl].astype(F32) + ob_ref[:, sl].astype(F32)
            ms = jnp.mean(o * o, axis=-1, keepdims=True)
            y = o * lax.rsqrt(ms + RMS_EPS) * g_ref[...]
            gate = gate_ref[:, sl].astype(F32)
            y = y * (gate * (1.0 / (1.0 + jnp.exp(-gate))))
            y_ref[:, base + hh * dv: base + (hh + 1) * dv] = y.astype(BF16)

    head_group(ofg_ref, obg_ref, gog_ref, gng_ref, GLA_HEADS, GLA_DV, 0)
    head_group(ofh_ref, obh_ref, hog_ref, hng_ref, HG_HEADS, HG_DIM, GLA_WIDTH)
    m = jnp.dot(y_ref[...], w_ref[...], preferred_element_type=F32)
    o_ref[...] = _layernorm(ALPHA * h_ref[...] + m, lng_ref[...], lnb_ref[...])


def _outproj(ofg, obg, ofh, obh, z, h, layer, w_out, gng, hng, lng, lnb):
    n = h.shape[0]
    tm = min(OUT_TM, n)
    half = lambda i: (i, 0)
    const = lambda i: (0, 0)
    return pl.pallas_call(
        _outproj_kernel,
        grid=(n // tm,),
        in_specs=[
            pl.BlockSpec((tm, GLA_WIDTH), half), pl.BlockSpec((tm, GLA_WIDTH), half),
            pl.BlockSpec((tm, HG_WIDTH), half), pl.BlockSpec((tm, HG_WIDTH), half),
            pl.BlockSpec((tm, GLA_WIDTH), lambda i: (i, 2048 // GLA_WIDTH)),
            pl.BlockSpec((tm, HG_WIDTH), lambda i: (i, 7168 // HG_WIDTH)),
            pl.BlockSpec((tm, D_MODEL), half),
            pl.BlockSpec((None, D_MODEL, D_MODEL), lambda i: (layer, 0, 0), pipeline_mode=pl.Buffered(1)),
            pl.BlockSpec((1, GLA_DV), const), pl.BlockSpec((1, HG_DIM), const),
            pl.BlockSpec((1, D_MODEL), const), pl.BlockSpec((1, D_MODEL), const),
        ],
        out_specs=pl.BlockSpec((tm, D_MODEL), half),
        out_shape=jax.ShapeDtypeStruct((n, D_MODEL), F32),
        scratch_shapes=[pltpu.VMEM((tm, D_MODEL), BF16)],
        compiler_params=pltpu.CompilerParams(
            dimension_semantics=("arbitrary",), vmem_limit_bytes=VMEM_LIMIT),
        name="outproj_ln1",
    )(ofg, obg, ofh, obh, z, z, h, w_out, gng, hng, lng, lnb)


def _mlp_kernel(h_ref, wup_ref, wdn_ref, lng_ref, lnb_ref, o_ref, xb_ref):
    j = pl.program_id(1)

    @pl.when(j == 0)
    def _():
        xb_ref[...] = h_ref[...].astype(BF16)
        o_ref[...] = jnp.zeros_like(o_ref)

    a = jnp.dot(xb_ref[...], wup_ref[...], preferred_element_type=F32)
    a = jnp.square(jnp.maximum(a, 0.0)).astype(BF16)
    o_ref[...] += jnp.dot(a, wdn_ref[...], preferred_element_type=F32)

    @pl.when(j == pl.num_programs(1) - 1)
    def _():
        o_ref[...] = _layernorm(ALPHA * h_ref[...] + o_ref[...], lng_ref[...], lnb_ref[...])


def _mlp(h, layer, w_up, w_down, lng, lnb):
    n = h.shape[0]
    tm = min(MLP_TM, n)
    return pl.pallas_call(
        _mlp_kernel,
        grid=(n // tm, D_FF // MLP_TF),
        in_specs=[
            pl.BlockSpec((tm, D_MODEL), lambda i, j: (i, 0)),
            pl.BlockSpec((None, D_MODEL, MLP_TF), lambda i, j: (layer, 0, j)),
            pl.BlockSpec((None, MLP_TF, D_MODEL), lambda i, j: (layer, j, 0)),
            pl.BlockSpec((1, D_MODEL), lambda i, j: (0, 0)),
            pl.BlockSpec((1, D_MODEL), lambda i, j: (0, 0)),
        ],
        out_specs=pl.BlockSpec((tm, D_MODEL), lambda i, j: (i, 0)),
        out_shape=jax.ShapeDtypeStruct((n, D_MODEL), F32),
        scratch_shapes=[pltpu.VMEM((tm, D_MODEL), BF16)],
        compiler_params=pltpu.CompilerParams(
            dimension_semantics=("arbitrary", "arbitrary"), vmem_limit_bytes=MLP_VMEM_LIMIT),
        name="mlp_ln2",
    )(h, w_up, w_down, lng, lnb)


def _lower_bounds(p):
    c = jnp.cumsum(jax.nn.softmax(p.astype(F32), axis=0), axis=0)
    return c - c[0:1]


def _prep_params(w_in, gla_w_lr2, gla_b_lr, gla_norm_g, hg_norm_g, lbs_f, lbs_b,
                 w_out, ln1_g, ln1_b, w_up, w_down, ln2_g, ln2_b):
    glr0 = 2 * GLA_KWIDTH + 2 * GLA_WIDTH
    w_main = jnp.concatenate([w_in[..., :glr0], w_in[..., glr0 + 2 * GLA_RANK:]], axis=-1).astype(BF16)
    w_glr = jnp.pad(w_in[..., glr0:glr0 + 2 * GLA_RANK],
                    ((0, 0), (0, 0), (0, GLR_PAD - 2 * GLA_RANK))).astype(BF16)
    stacked = dict(w_main=w_main, w_glr=w_glr, w_out=w_out.astype(BF16),
                   w_up=w_up.astype(BF16), w_down=w_down.astype(BF16))
    layers = []
    for l in range(DEPTH):
        w2 = gla_w_lr2[l].astype(F32)
        w2c = jnp.zeros((GLR_PAD, 2 * GLA_KWIDTH), F32)
        w2c = w2c.at[:GLA_RANK, :GLA_KWIDTH].set(w2[0]).at[GLA_RANK:2 * GLA_RANK, GLA_KWIDTH:].set(w2[1])
        lb = jnp.concatenate([lbs_f[l], lbs_b[l]])[None, :]
        row = lambda a: a[l].astype(F32)[None, :]
        layers.append(dict(
            w2c=w2c.astype(BF16), b2c=gla_b_lr[l].astype(F32).reshape(1, 2 * GLA_KWIDTH),
            gate_c=jnp.concatenate([lb, 1.0 - lb, jnp.log(lb), jnp.log1p(-lb),
                                    jnp.zeros((4, 2 * HG_WIDTH), F32)], axis=0),
            gng=row(gla_norm_g), hng=row(hg_norm_g), ln1_g=row(ln1_g), ln1_b=row(ln1_b),
            ln2_g=row(ln2_g), ln2_b=row(ln2_b)))
    return stacked, layers


def _trunk(x, stacked, layers):
    batch, seq, _ = x.shape
    h = x.reshape(batch * seq, D_MODEL).astype(F32)
    for l, p in enumerate(layers):
        z, logf, ga = _inproj(h, l, stacked["w_main"], stacked["w_glr"], p["w2c"], p["b2c"], p["gate_c"])
        ofg, obg = _scan((z, 0), (z, 1), (z, 1), (z, 1), (ga, 0), (ga, 1),
                         batch=batch, seq=seq, heads=GLA_HEADS, dk=GLA_DK, dv=GLA_DV, chunk=GLA_CHUNK)
        ofh, obh = _scan((z, 3), (z, 4), (z, 5), (z, 6), (logf, 0), (logf, 1),
                         batch=batch, seq=seq, heads=HG_HEADS, dk=HG_DIM, dv=HG_DIM, chunk=HG_CHUNK)
        h = _outproj(ofg, obg, ofh, obh, z, h, l, stacked["w_out"], p["gng"], p["hng"], p["ln1_g"], p["ln1_b"])
        h = _mlp(h, l, stacked["w_up"], stacked["w_down"], p["ln2_g"], p["ln2_b"])
    return h.reshape(batch, seq, D_MODEL).astype(x.dtype)


def kernel(x_prompt, x_sample, w_in, gla_w_lr2, gla_b_lr, gla_norm_g, hg_norm_g, lower_bounds,
           w_out, ln1_g, ln1_b, w_up, w_down, ln2_g, ln2_b):
    lbs_f = _lower_bounds(lower_bounds[0])
    lbs_b = _lower_bounds(lower_bounds[1])
    stacked, layers = _prep_params(w_in, gla_w_lr2, gla_b_lr, gla_norm_g, hg_norm_g, lbs_f, lbs_b,
                                   w_out, ln1_g, ln1_b, w_up, w_down, ln2_g, ln2_b)
    return (_trunk(x_prompt, stacked, layers), _trunk(x_sample, stacked, layers))
```

```python
import functools

import jax
import jax.numpy as jnp
from jax import lax
from jax.experimental import pallas as pl
from jax.experimental.pallas import tpu as pltpu

F32 = jnp.float32
BF16 = jnp.bfloat16

D_MODEL = 2048
DEPTH = 2
GLA_HEADS = 4
GLA_WIDTH = D_MODEL // 2
GLA_DV = GLA_WIDTH // GLA_HEADS
GLA_DK = GLA_DV // 2
GLA_KWIDTH = GLA_HEADS * GLA_DK
GLA_RANK = 16
GLA_TAU = 16.0
HG_DIM = 128
HG_WIDTH = D_MODEL - GLA_WIDTH
HG_HEADS = HG_WIDTH // HG_DIM
D_FF = 4 * D_MODEL
LN_EPS = 1e-5
RMS_EPS = 1e-6
F32_TINY = 1.1754943508222875e-38
ALPHA = (2.0 * DEPTH) ** 0.25

LANES = 128
GLR_PAD = LANES
Z_WIDTH = 8192
VMEM_LIMIT = 48 * 1024 * 1024
IN_VMEM_LIMIT = 58 * 1024 * 1024

IN_TN = 1024
IN_TM = 1024
MXU_COLS = 256
GATE_TILE_LO = 4096 // IN_TN
GATE_TILE_HI = 6144 // IN_TN

GLA_CHUNK = 256
HG_CHUNK = 128
SCAN_CHUNKS_PER_STEP = 2
MAX_HALF_RANGE = 80.0

OUT_TM = 512
MLP_TM = 1024
MLP_TF = 512
MLP_VMEM_LIMIT = 58 * 1024 * 1024

_NT = (((1,), (1,)), ((), ()))
_TN = (((0,), (0,)), ((), ()))


def _log_sigmoid(x):
    return jnp.minimum(x, 0.0) - jnp.log(1.0 + jnp.exp(-jnp.abs(x)))


def _layernorm(r, g, b):
    mu = jnp.mean(r, axis=-1, keepdims=True)
    d = r - mu
    var = jnp.mean(d * d, axis=-1, keepdims=True)
    return d * lax.rsqrt(var + LN_EPS) * g + b


def _inproj_kernel(x_ref, w_ref, wglr_ref, w2_ref, b2_ref, gc_ref, z_ref, logf_ref, ga_ref, xb_ref):
    j = pl.program_id(1)
    is_gate = jnp.logical_and(j >= GATE_TILE_LO, j < GATE_TILE_HI)
    col_groups = lambda width: [slice(c, c + MXU_COLS) for c in range(0, width, MXU_COLS)]

    last = pl.num_programs(1) - 1

    def project_plain():
        for cols in col_groups(z_ref.shape[1]):
            z_ref[:, cols] = jnp.dot(xb_ref[...], w_ref[:, cols], preferred_element_type=F32).astype(BF16)

    @pl.when(j == 0)
    def _():
        xb = x_ref[...].astype(BF16)
        xb_ref[...] = xb
        for cols in col_groups(z_ref.shape[1]):
            zc = jnp.dot(xb, w_ref[:, cols], preferred_element_type=F32)
            if cols.stop <= GLA_KWIDTH:
                zc = zc * (GLA_DK ** -0.5)
            z_ref[:, cols] = zc.astype(BF16)

    @pl.when(jnp.logical_and(jnp.logical_and(j > 0, j < last), jnp.logical_not(is_gate)))
    def _():
        project_plain()

    @pl.when(j == last)
    def _():
        project_plain()
        glr = jnp.dot(xb_ref[...], wglr_ref[...], preferred_element_type=F32).astype(BF16)
        for cols in col_groups(ga_ref.shape[1]):
            a = jnp.dot(glr, w2_ref[:, cols], preferred_element_type=F32) + b2_ref[:, cols]
            ga_ref[:, cols] = _log_sigmoid(a) * (1.0 / GLA_TAU)

    @pl.when(is_gate)
    def _():
        f_min = None
        half = x_ref.shape[0] // 2
        for cols in col_groups(z_ref.shape[1]):
            lb = gc_ref[0:1, cols]
            one_minus_lb = gc_ref[1:2, cols]
            for rows in (slice(0, half), slice(half, 2 * half)):
                acc = jnp.dot(xb_ref[rows, :], w_ref[:, cols], preferred_element_type=F32)
                e = jnp.exp(-jnp.abs(acc))
                r = 1.0 / (1.0 + e)
                er = e * r
                pos = acc >= 0.0
                f = lb + one_minus_lb * jnp.where(pos, r, er)
                logf_ref[rows, cols] = jnp.log(f)
                z_ref[rows, cols] = (one_minus_lb * jnp.where(pos, er, r)).astype(BF16)
                f_min = f if f_min is None else jnp.minimum(f_min, f)

        @pl.when(jnp.logical_not(jnp.min(f_min) >= F32_TINY))
        def _():
            acc = jnp.dot(xb_ref[...], w_ref[...], preferred_element_type=F32)
            log_lb = gc_ref[2:3, :]
            x2 = gc_ref[3:4, :] + _log_sigmoid(acc)
            logf_ref[...] = (jnp.maximum(log_lb, x2)
                             + jnp.log(1.0 + jnp.exp(-jnp.abs(log_lb - x2))))


def _inproj(x, layer, w_main, w_glr, w2c, b2c, gate_c):
    n = x.shape[0]
    tm = min(IN_TM, n)
    n_gate = GATE_TILE_HI - GATE_TILE_LO
    gate_idx = lambda i, j: jnp.clip(j - GATE_TILE_LO, 0, n_gate - 1)
    return pl.pallas_call(
        _inproj_kernel,
        grid=(n // tm, Z_WIDTH // IN_TN),
        in_specs=[
            pl.BlockSpec((tm, D_MODEL), lambda i, j: (i, 0)),
            pl.BlockSpec((None, D_MODEL, IN_TN), lambda i, j: (layer, 0, j)),
            pl.BlockSpec((None, D_MODEL, GLR_PAD), lambda i, j: (layer, 0, 0)),
            pl.BlockSpec((GLR_PAD, 2 * GLA_KWIDTH), lambda i, j: (0, 0)),
            pl.BlockSpec((1, 2 * GLA_KWIDTH), lambda i, j: (0, 0)),
            pl.BlockSpec((8, IN_TN), lambda i, j: (0, gate_idx(i, j))),
        ],
        out_specs=[
            pl.BlockSpec((tm, IN_TN), lambda i, j: (i, j)),
            pl.BlockSpec((tm, IN_TN), lambda i, j: (i, gate_idx(i, j))),
            pl.BlockSpec((tm, 2 * GLA_KWIDTH), lambda i, j: (i, 0)),
        ],
        out_shape=[
            jax.ShapeDtypeStruct((n, Z_WIDTH), BF16),
            jax.ShapeDtypeStruct((n, 2 * HG_WIDTH), F32),
            jax.ShapeDtypeStruct((n, 2 * GLA_KWIDTH), F32),
        ],
        scratch_shapes=[pltpu.VMEM((tm, D_MODEL), BF16)],
        compiler_params=pltpu.CompilerParams(
            dimension_semantics=("arbitrary", "arbitrary"), vmem_limit_bytes=IN_VMEM_LIMIT),
        name="inproj",
    )(x, w_main, w_glr, w2c, b2c, gate_c)


def _cumsum_rows(tri, g):
    hi = g.astype(BF16)
    lo = (g - hi.astype(F32)).astype(BF16)
    return jnp.dot(jnp.concatenate([tri, tri], axis=1), jnp.concatenate([hi, lo], axis=0),
                   preferred_element_type=F32)


def _dir_fast(q, k, v, b, tot, st, mask):
    m = 0.5 * tot
    qt = q * jnp.exp(b - m).astype(BF16)
    kt = k * jnp.exp(m - b).astype(BF16)
    c = q.shape[0]
    ssc = (st * jnp.exp(m)).astype(BF16)
    r = lax.dot_general(qt, jnp.concatenate([kt, ssc], axis=0), _NT, preferred_element_type=F32)
    s = jnp.where(mask, r[:, :c], 0.0).astype(BF16)
    o = jnp.dot(s, v, preferred_element_type=F32) + r[:, c:]
    u = lax.dot_general(v, kt, _TN, preferred_element_type=F32)
    return o, st * jnp.exp(tot) + u * jnp.exp(tot - m)


def _dir_slow(q, k, v, b, tot, st, reverse, row_ref):
    c = q.shape[0]
    q = q.astype(F32)
    k = k.astype(F32)
    row_ref[0] = b
    row_ref[1] = q
    jj = lax.broadcasted_iota(jnp.int32, (c, 1), 0)
    lane = lax.broadcasted_iota(jnp.int32, (c, c), 1)

    def body(i, s_t):
        bi = row_ref[0, pl.ds(i, 1), :]
        qi = row_ref[1, pl.ds(i, 1), :]
        valid = (jj >= i) if reverse else (jj <= i)
        w = jnp.where(valid, jnp.exp(jnp.minimum(bi - b, 0.0)), 0.0)
        col = jnp.sum(qi * k * w, axis=-1, keepdims=True)
        return jnp.where(lane == i, col, s_t)

    s_t = lax.fori_loop(0, c, body, jnp.zeros((c, c), F32))
    o = lax.dot_general(s_t.astype(BF16), v, _TN, preferred_element_type=F32)
    qin = (q * jnp.exp(b)).astype(BF16)
    o = o + lax.dot_general(qin, st.astype(BF16), _NT, preferred_element_type=F32)
    kout = (k * jnp.exp(tot - b)).astype(BF16)
    u = lax.dot_general(v, kout, _TN, preferred_element_type=F32)
    return o, st * jnp.exp(tot) + u


def _scan_kernel(qf_ref, kf_ref, vf_ref, gf_ref, qb_ref, kb_ref, vb_ref, gb_ref,
                 of_ref, ob_ref, sf_ref, sb_ref, row_ref, *, heads, dk, dv, c):
    @pl.when(pl.program_id(1) == 0)
    def _():
        sf_ref[...] = jnp.zeros_like(sf_ref)
        sb_ref[...] = jnp.zeros_like(sb_ref)

    row = lax.broadcasted_iota(jnp.int32, (c, c), 0)
    col = lax.broadcasted_iota(jnp.int32, (c, c), 1)
    lower = row >= col
    upper = row <= col

    def chunk_step(rf, rb):
        bf_all = _cumsum_rows(jnp.where(lower, 1.0, 0.0).astype(BF16), gf_ref[rf, :])
        bb_all = _cumsum_rows(jnp.where(upper, 1.0, 0.0).astype(BF16), gb_ref[rb, :])
        tot_f = bf_all[c - 1:c, :]
        tot_b = bb_all[0:1, :]
        span = jnp.maximum(jnp.max(-tot_f), jnp.max(-tot_b))
        fast = span < 2.0 * MAX_HALF_RANGE

        def run(step_f, step_b):
            for h in range(heads):
                ks = slice(h * dk, (h + 1) * dk)
                vs = slice(h * dv, (h + 1) * dv)
                o, s_new = step_f(qf_ref[rf, ks], kf_ref[rf, ks], vf_ref[rf, vs],
                                  bf_all[:, ks], tot_f[:, ks], sf_ref[h])
                of_ref[rf, vs] = o.astype(BF16)
                sf_ref[h] = s_new
                o, s_new = step_b(qb_ref[rb, ks], kb_ref[rb, ks], vb_ref[rb, vs],
                                  bb_all[:, ks], tot_b[:, ks], sb_ref[h])
                ob_ref[rb, vs] = o.astype(BF16)
                sb_ref[h] = s_new

        @pl.when(fast)
        def _():
            run(functools.partial(_dir_fast, mask=lower), functools.partial(_dir_fast, mask=upper))

        @pl.when(jnp.logical_not(fast))
        def _():
            run(functools.partial(_dir_slow, reverse=False, row_ref=row_ref),
                functools.partial(_dir_slow, reverse=True, row_ref=row_ref))

    n_sub = qf_ref.shape[0] // c
    for sub in range(n_sub):
        chunk_step(slice(sub * c, (sub + 1) * c), slice((n_sub - 1 - sub) * c, (n_sub - sub) * c))


def _scan(q, kf, kb, v, gf, gb, *, batch, seq, heads, dk, dv, chunk):
    n = batch * seq
    c = min(chunk, seq)
    rows = min(SCAN_CHUNKS_PER_STEP * c, seq)
    nb = seq // rows
    kw, vw = heads * dk, heads * dv
    fwd = lambda blk: (lambda b, i: (b * nb + i, blk))
    bwd = lambda blk: (lambda b, i: (b * nb + (nb - 1 - i), blk))
    out_f = lambda b, i: (b * nb + i, 0)
    out_b = lambda b, i: (b * nb + (nb - 1 - i), 0)
    return pl.pallas_call(
        functools.partial(_scan_kernel, heads=heads, dk=dk, dv=dv, c=c),
        grid=(batch, nb),
        in_specs=[
            pl.BlockSpec((rows, kw), fwd(q[1])), pl.BlockSpec((rows, kw), fwd(kf[1])),
            pl.BlockSpec((rows, vw), fwd(v[1])), pl.BlockSpec((rows, kw), fwd(gf[1])),
            pl.BlockSpec((rows, kw), bwd(q[1])), pl.BlockSpec((rows, kw), bwd(kb[1])),
            pl.BlockSpec((rows, vw), bwd(v[1])), pl.BlockSpec((rows, kw), bwd(gb[1])),
        ],
        out_specs=[pl.BlockSpec((rows, vw), out_f), pl.BlockSpec((rows, vw), out_b)],
        out_shape=[jax.ShapeDtypeStruct((n, vw), BF16), jax.ShapeDtypeStruct((n, vw), BF16)],
        scratch_shapes=[
            pltpu.VMEM((heads, dv, dk), F32),
            pltpu.VMEM((heads, dv, dk), F32),
            pltpu.VMEM((2, c, dk), F32),
        ],
        compiler_params=pltpu.CompilerParams(
            dimension_semantics=("arbitrary", "arbitrary"), vmem_limit_bytes=VMEM_LIMIT),
        name=f"scan_h{heads}",
    )(q[0], kf[0], v[0], gf[0], q[0], kb[0], v[0], gb[0])


def _outproj_kernel(ofg_ref, obg_ref, ofh_ref, obh_ref, gog_ref, hog_ref, h_ref, w_ref,
                    gng_ref, hng_ref, lng_ref, lnb_ref, o_ref, y_ref):
    def head_group(of_ref, ob_ref, gate_ref, g_ref, heads, dv, base):
        for hh in range(heads):
            sl = slice(hh * dv, (hh + 1) * dv)
            o = of_ref[:, sl].astype(F32) + ob_ref[:, sl].astype(F32)
            ms = jnp.mean(o * o, axis=-1, keepdims=True)
            y = o * lax.rsqrt(ms + RMS_EPS) * g_ref[...]
            gate = gate_ref[:, sl].astype(F32)
            y = y * (gate * (1.0 / (1.0 + jnp.exp(-gate))))
            y_ref[:, base + hh * dv: base + (hh + 1) * dv] = y.astype(BF16)

    head_group(ofg_ref, obg_ref, gog_ref, gng_ref, GLA_HEADS, GLA_DV, 0)
    head_group(ofh_ref, obh_ref, hog_ref, hng_ref, HG_HEADS, HG_DIM, GLA_WIDTH)
    m = jnp.dot(y_ref[...], w_ref[...], preferred_element_type=F32)
    o_ref[...] = _layernorm(ALPHA * h_ref[...] + m, lng_ref[...], lnb_ref[...])


def _outproj(ofg, obg, ofh, obh, z, h, layer, w_out, gng, hng, lng, lnb):
    n = h.shape[0]
    tm = min(OUT_TM, n)
    half = lambda i: (i, 0)
    const = lambda i: (0, 0)
    return pl.pallas_call(
        _outproj_kernel,
        grid=(n // tm,),
        in_specs=[
            pl.BlockSpec((tm, GLA_WIDTH), half), pl.BlockSpec((tm, GLA_WIDTH), half),
            pl.BlockSpec((tm, HG_WIDTH), half), pl.BlockSpec((tm, HG_WIDTH), half),
            pl.BlockSpec((tm, GLA_WIDTH), lambda i: (i, 2048 // GLA_WIDTH)),
            pl.BlockSpec((tm, HG_WIDTH), lambda i: (i, 7168 // HG_WIDTH)),
            pl.BlockSpec((tm, D_MODEL), half),
            pl.BlockSpec((None, D_MODEL, D_MODEL), lambda i: (layer, 0, 0), pipeline_mode=pl.Buffered(1)),
            pl.BlockSpec((1, GLA_DV), const), pl.BlockSpec((1, HG_DIM), const),
            pl.BlockSpec((1, D_MODEL), const), pl.BlockSpec((1, D_MODEL), const),
        ],
        out_specs=pl.BlockSpec((tm, D_MODEL), half),
        out_shape=jax.ShapeDtypeStruct((n, D_MODEL), F32),
        scratch_shapes=[pltpu.VMEM((tm, D_MODEL), BF16)],
        compiler_params=pltpu.CompilerParams(
            dimension_semantics=("arbitrary",), vmem_limit_bytes=VMEM_LIMIT),
        name="outproj_ln1",
    )(ofg, obg, ofh, obh, z, z, h, w_out, gng, hng, lng, lnb)


def _mlp_kernel(h_ref, wup_ref, wdn_ref, lng_ref, lnb_ref, o_ref, xb_ref):
    j = pl.program_id(1)

    @pl.when(j == 0)
    def _():
        xb_ref[...] = h_ref[...].astype(BF16)
        o_ref[...] = jnp.zeros_like(o_ref)

    a = jnp.dot(xb_ref[...], wup_ref[...], preferred_element_type=F32)
    a = jnp.square(jnp.maximum(a, 0.0)).astype(BF16)
    o_ref[...] += jnp.dot(a, wdn_ref[...], preferred_element_type=F32)

    @pl.when(j == pl.num_programs(1) - 1)
    def _():
        o_ref[...] = _layernorm(ALPHA * h_ref[...] + o_ref[...], lng_ref[...], lnb_ref[...])


def _mlp(h, layer, w_up, w_down, lng, lnb):
    n = h.shape[0]
    tm = min(MLP_TM, n)
    return pl.pallas_call(
        _mlp_kernel,
        grid=(n // tm, D_FF // MLP_TF),
        in_specs=[
            pl.BlockSpec((tm, D_MODEL), lambda i, j: (i, 0)),
            pl.BlockSpec((None, D_MODEL, MLP_TF), lambda i, j: (layer, 0, j)),
            pl.BlockSpec((None, MLP_TF, D_MODEL), lambda i, j: (layer, j, 0)),
            pl.BlockSpec((1, D_MODEL), lambda i, j: (0, 0)),
            pl.BlockSpec((1, D_MODEL), lambda i, j: (0, 0)),
        ],
        out_specs=pl.BlockSpec((tm, D_MODEL), lambda i, j: (i, 0)),
        out_shape=jax.ShapeDtypeStruct((n, D_MODEL), F32),
        scratch_shapes=[pltpu.VMEM((tm, D_MODEL), BF16)],
        compiler_params=pltpu.CompilerParams(
            dimension_semantics=("arbitrary", "arbitrary"), vmem_limit_bytes=MLP_VMEM_LIMIT),
        name="mlp_ln2",
    )(h, w_up, w_down, lng, lnb)


def _lower_bounds(p):
    c = jnp.cumsum(jax.nn.softmax(p.astype(F32), axis=0), axis=0)
    return c - c[0:1]


def _prep_params(w_in, gla_w_lr2, gla_b_lr, gla_norm_g, hg_norm_g, lbs_f, lbs_b,
                 w_out, ln1_g, ln1_b, w_up, w_down, ln2_g, ln2_b):
    glr0 = 2 * GLA_KWIDTH + 2 * GLA_WIDTH
    w_main = jnp.concatenate([w_in[..., :glr0], w_in[..., glr0 + 2 * GLA_RANK:]], axis=-1).astype(BF16)
    w_glr = jnp.pad(w_in[..., glr0:glr0 + 2 * GLA_RANK],
                    ((0, 0), (0, 0), (0, GLR_PAD - 2 * GLA_RANK))).astype(BF16)
    stacked = dict(w_main=w_main, w_glr=w_glr, w_out=w_out.astype(BF16),
                   w_up=w_up.astype(BF16), w_down=w_down.astype(BF16))
    layers = []
    for l in range(DEPTH):
        w2 = gla_w_lr2[l].astype(F32)
        w2c = jnp.zeros((GLR_PAD, 2 * GLA_KWIDTH), F32)
        w2c = w2c.at[:GLA_RANK, :GLA_KWIDTH].set(w2[0]).at[GLA_RANK:2 * GLA_RANK, GLA_KWIDTH:].set(w2[1])
        lb = jnp.concatenate([lbs_f[l], lbs_b[l]])[None, :]
        row = lambda a: a[l].astype(F32)[None, :]
        layers.append(dict(
            w2c=w2c.astype(BF16), b2c=gla_b_lr[l].astype(F32).reshape(1, 2 * GLA_KWIDTH),
            gate_c=jnp.concatenate([lb, 1.0 - lb, jnp.log(lb), jnp.log1p(-lb),
                                    jnp.zeros((4, 2 * HG_WIDTH), F32)], axis=0),
            gng=row(gla_norm_g), hng=row(hg_norm_g), ln1_g=row(ln1_g), ln1_b=row(ln1_b),
            ln2_g=row(ln2_g), ln2_b=row(ln2_b)))
    return stacked, layers


def _trunk(x, stacked, layers):
    batch, seq, _ = x.shape
    h = x.reshape(batch * seq, D_MODEL).astype(F32)
    for l, p in enumerate(layers):
        z, logf, ga = _inproj(h, l, stacked["w_main"], stacked["w_glr"], p["w2c"], p["b2c"], p["gate_c"])
        ofg, obg = _scan((z, 0), (z, 1), (z, 1), (z, 1), (ga, 0), (ga, 1),
                         batch=batch, seq=seq, heads=GLA_HEADS, dk=GLA_DK, dv=GLA_DV, chunk=GLA_CHUNK)
        ofh, obh = _scan((z, 3), (z, 4), (z, 5), (z, 6), (logf, 0), (logf, 1),
                         batch=batch, seq=seq, heads=HG_HEADS, dk=HG_DIM, dv=HG_DIM, chunk=HG_CHUNK)
        h = _outproj(ofg, obg, ofh, obh, z, h, l, stacked["w_out"], p["gng"], p["hng"], p["ln1_g"], p["ln1_b"])
        h = _mlp(h, l, stacked["w_up"], stacked["w_down"], p["ln2_g"], p["ln2_b"])
    return h.reshape(batch, seq, D_MODEL).astype(x.dtype)


def kernel(x_prompt, x_sample, w_in, gla_w_lr2, gla_b_lr, gla_norm_g, hg_norm_g, lower_bounds,
           w_out, ln1_g, ln1_b, w_up, w_down, ln2_g, ln2_b):
    lbs_f = _lower_bounds(lower_bounds[0])
    lbs_b = _lower_bounds(lower_bounds[1])
    stacked, layers = _prep_params(w_in, gla_w_lr2, gla_b_lr, gla_norm_g, hg_norm_g, lbs_f, lbs_b,
                                   w_out, ln1_g, ln1_b, w_up, w_down, ln2_g, ln2_b)
    return (_trunk(x_prompt, stacked, layers), _trunk(x_sample, stacked, layers))
```

```python
import functools

import jax
import jax.numpy as jnp
from jax import lax
from jax.experimental import pallas as pl
from jax.experimental.pallas import tpu as pltpu

F32 = jnp.float32
BF16 = jnp.bfloat16

D_MODEL = 2048
DEPTH = 2
GLA_HEADS = 4
GLA_WIDTH = D_MODEL // 2
GLA_DV = GLA_WIDTH // GLA_HEADS
GLA_DK = GLA_DV // 2
GLA_KWIDTH = GLA_HEADS * GLA_DK
GLA_RANK = 16
GLA_TAU = 16.0
HG_DIM = 128
HG_WIDTH = D_MODEL - GLA_WIDTH
HG_HEADS = HG_WIDTH // HG_DIM
D_FF = 4 * D_MODEL
LN_EPS = 1e-5
RMS_EPS = 1e-6
F32_TINY = 1.1754943508222875e-38
ALPHA = (2.0 * DEPTH) ** 0.25

LANES = 128
GLR_PAD = LANES
Z_WIDTH = 8192
VMEM_LIMIT = 48 * 1024 * 1024
IN_VMEM_LIMIT = 58 * 1024 * 1024

IN_TN = 1024
IN_TM = 1024
MXU_COLS = 256
GATE_TILE_LO = 4096 // IN_TN
GATE_TILE_HI = 6144 // IN_TN

GLA_CHUNK = 256
HG_CHUNK = 128
SCAN_CHUNKS_PER_STEP = 4
MAX_HALF_RANGE = 80.0

OUT_TM = 512
MLP_TM = 1024
MLP_TF = 1024
LN_ROWS = 128
MLP_VMEM_LIMIT = 60 * 1024 * 1024

_NT = (((1,), (1,)), ((), ()))
_TN = (((0,), (0,)), ((), ()))


def _log_sigmoid(x):
    return jnp.minimum(x, 0.0) - jnp.log(1.0 + jnp.exp(-jnp.abs(x)))


def _layernorm(r, g, b):
    mu = jnp.mean(r, axis=-1, keepdims=True)
    d = r - mu
    var = jnp.mean(d * d, axis=-1, keepdims=True)
    return d * lax.rsqrt(var + LN_EPS) * g + b


def _inproj_kernel(x_ref, w_ref, wglr_ref, w2_ref, b2_ref, gc_ref, z_ref, logf_ref, ga_ref, xb_ref):
    j = pl.program_id(1)
    is_gate = jnp.logical_and(j >= GATE_TILE_LO, j < GATE_TILE_HI)
    col_groups = lambda width: [slice(c, c + MXU_COLS) for c in range(0, width, MXU_COLS)]

    last = pl.num_programs(1) - 1

    def project_plain():
        for cols in col_groups(z_ref.shape[1]):
            z_ref[:, cols] = jnp.dot(xb_ref[...], w_ref[:, cols], preferred_element_type=F32).astype(BF16)

    @pl.when(j == 0)
    def _():
        xb = x_ref[...].astype(BF16)
        xb_ref[...] = xb
        for cols in col_groups(z_ref.shape[1]):
            zc = jnp.dot(xb, w_ref[:, cols], preferred_element_type=F32)
            if cols.stop <= GLA_KWIDTH:
                zc = zc * (GLA_DK ** -0.5)
            z_ref[:, cols] = zc.astype(BF16)

    @pl.when(jnp.logical_and(jnp.logical_and(j > 0, j < last), jnp.logical_not(is_gate)))
    def _():
        project_plain()

    @pl.when(j == last)
    def _():
        project_plain()
        glr = jnp.dot(xb_ref[...], wglr_ref[...], preferred_element_type=F32).astype(BF16)
        for cols in col_groups(ga_ref.shape[1]):
            a = jnp.dot(glr, w2_ref[:, cols], preferred_element_type=F32) + b2_ref[:, cols]
            ga_ref[:, cols] = _log_sigmoid(a) * (1.0 / GLA_TAU)

    @pl.when(is_gate)
    def _():
        f_min = None
        half = x_ref.shape[0] // 2
        for cols in col_groups(z_ref.shape[1]):
            lb = gc_ref[0:1, cols]
            one_minus_lb = gc_ref[1:2, cols]
            for rows in (slice(0, half), slice(half, 2 * half)):
                acc = jnp.dot(xb_ref[rows, :], w_ref[:, cols], preferred_element_type=F32)
                e = jnp.exp(-jnp.abs(acc))
                r = 1.0 / (1.0 + e)
                er = e * r
                pos = acc >= 0.0
                f = lb + one_minus_lb * jnp.where(pos, r, er)
                logf_ref[rows, cols] = jnp.log(f)
                z_ref[rows, cols] = (one_minus_lb * jnp.where(pos, er, r)).astype(BF16)
                f_min = f if f_min is None else jnp.minimum(f_min, f)

        @pl.when(jnp.logical_not(jnp.min(f_min) >= F32_TINY))
        def _():
            acc = jnp.dot(xb_ref[...], w_ref[...], preferred_element_type=F32)
            log_lb = gc_ref[2:3, :]
            x2 = gc_ref[3:4, :] + _log_sigmoid(acc)
            logf_ref[...] = (jnp.maximum(log_lb, x2)
                             + jnp.log(1.0 + jnp.exp(-jnp.abs(log_lb - x2))))


def _inproj(x, layer, w_main, w_glr, w2c, b2c, gate_c):
    n = x.shape[0]
    tm = min(IN_TM, n)
    n_gate = GATE_TILE_HI - GATE_TILE_LO
    gate_idx = lambda i, j: jnp.clip(j - GATE_TILE_LO, 0, n_gate - 1)
    return pl.pallas_call(
        _inproj_kernel,
        grid=(n // tm, Z_WIDTH // IN_TN),
        in_specs=[
            pl.BlockSpec((tm, D_MODEL), lambda i, j: (i, 0)),
            pl.BlockSpec((None, D_MODEL, IN_TN), lambda i, j: (layer, 0, j)),
            pl.BlockSpec((None, D_MODEL, GLR_PAD), lambda i, j: (layer, 0, 0)),
            pl.BlockSpec((GLR_PAD, 2 * GLA_KWIDTH), lambda i, j: (0, 0)),
            pl.BlockSpec((1, 2 * GLA_KWIDTH), lambda i, j: (0, 0)),
            pl.BlockSpec((8, IN_TN), lambda i, j: (0, gate_idx(i, j))),
        ],
        out_specs=[
            pl.BlockSpec((tm, IN_TN), lambda i, j: (i, j)),
            pl.BlockSpec((tm, IN_TN), lambda i, j: (i, gate_idx(i, j))),
            pl.BlockSpec((tm, 2 * GLA_KWIDTH), lambda i, j: (i, 0)),
        ],
        out_shape=[
            jax.ShapeDtypeStruct((n, Z_WIDTH), BF16),
            jax.ShapeDtypeStruct((n, 2 * HG_WIDTH), F32),
            jax.ShapeDtypeStruct((n, 2 * GLA_KWIDTH), F32),
        ],
        scratch_shapes=[pltpu.VMEM((tm, D_MODEL), BF16)],
        compiler_params=pltpu.CompilerParams(
            dimension_semantics=("arbitrary", "arbitrary"), vmem_limit_bytes=IN_VMEM_LIMIT),
        name="inproj",
    )(x, w_main, w_glr, w2c, b2c, gate_c)


def _cumsum_rows(tri, g):
    hi = g.astype(BF16)
    lo = (g - hi.astype(F32)).astype(BF16)
    return jnp.dot(jnp.concatenate([tri, tri], axis=1), jnp.concatenate([hi, lo], axis=0),
                   preferred_element_type=F32)


def _dir_fast(q, k, v, b, tot, st, mask):
    m = 0.5 * tot
    qt = q * jnp.exp(b - m).astype(BF16)
    kt = k * jnp.exp(m - b).astype(BF16)
    c = q.shape[0]
    ssc = (st * jnp.exp(m)).astype(BF16)
    r = lax.dot_general(qt, jnp.concatenate([kt, ssc], axis=0), _NT, preferred_element_type=F32)
    s = jnp.where(mask, r[:, :c], 0.0).astype(BF16)
    o = jnp.dot(s, v, preferred_element_type=F32) + r[:, c:]
    u = lax.dot_general(v, kt, _TN, preferred_element_type=F32)
    return o, st * jnp.exp(tot) + u * jnp.exp(tot - m)


def _dir_slow(q, k, v, b, tot, st, reverse, row_ref):
    c = q.shape[0]
    q = q.astype(F32)
    k = k.astype(F32)
    row_ref[0] = b
    row_ref[1] = q
    jj = lax.broadcasted_iota(jnp.int32, (c, 1), 0)
    lane = lax.broadcasted_iota(jnp.int32, (c, c), 1)

    def body(i, s_t):
        bi = row_ref[0, pl.ds(i, 1), :]
        qi = row_ref[1, pl.ds(i, 1), :]
        valid = (jj >= i) if reverse else (jj <= i)
        w = jnp.where(valid, jnp.exp(jnp.minimum(bi - b, 0.0)), 0.0)
        col = jnp.sum(qi * k * w, axis=-1, keepdims=True)
        return jnp.where(lane == i, col, s_t)

    s_t = lax.fori_loop(0, c, body, jnp.zeros((c, c), F32))
    o = lax.dot_general(s_t.astype(BF16), v, _TN, preferred_element_type=F32)
    qin = (q * jnp.exp(b)).astype(BF16)
    o = o + lax.dot_general(qin, st.astype(BF16), _NT, preferred_element_type=F32)
    kout = (k * jnp.exp(tot - b)).astype(BF16)
    u = lax.dot_general(v, kout, _TN, preferred_element_type=F32)
    return o, st * jnp.exp(tot) + u


def _scan_kernel(qf_ref, kf_ref, vf_ref, gf_ref, qb_ref, kb_ref, vb_ref, gb_ref,
                 of_ref, ob_ref, sf_ref, sb_ref, row_ref, *, heads, dk, dv, c):
    @pl.when(pl.program_id(1) == 0)
    def _():
        sf_ref[...] = jnp.zeros_like(sf_ref)
        sb_ref[...] = jnp.zeros_like(sb_ref)

    row = lax.broadcasted_iota(jnp.int32, (c, c), 0)
    col = lax.broadcasted_iota(jnp.int32, (c, c), 1)
    lower = row >= col
    upper = row <= col

    def chunk_step(rf, rb):
        bf_all = _cumsum_rows(jnp.where(lower, 1.0, 0.0).astype(BF16), gf_ref[rf, :])
        bb_all = _cumsum_rows(jnp.where(upper, 1.0, 0.0).astype(BF16), gb_ref[rb, :])
        tot_f = bf_all[c - 1:c, :]
        tot_b = bb_all[0:1, :]
        span = jnp.maximum(jnp.max(-tot_f), jnp.max(-tot_b))
        fast = span < 2.0 * MAX_HALF_RANGE

        def run(step_f, step_b):
            for h in range(heads):
                ks = slice(h * dk, (h + 1) * dk)
                vs = slice(h * dv, (h + 1) * dv)
                o, s_new = step_f(qf_ref[rf, ks], kf_ref[rf, ks], vf_ref[rf, vs],
                                  bf_all[:, ks], tot_f[:, ks], sf_ref[h])
                of_ref[rf, vs] = o.astype(BF16)
                sf_ref[h] = s_new
                o, s_new = step_b(qb_ref[rb, ks], kb_ref[rb, ks], vb_ref[rb, vs],
                                  bb_all[:, ks], tot_b[:, ks], sb_ref[h])
                ob_ref[rb, vs] = o.astype(BF16)
                sb_ref[h] = s_new

        @pl.when(fast)
        def _():
            run(functools.partial(_dir_fast, mask=lower), functools.partial(_dir_fast, mask=upper))

        @pl.when(jnp.logical_not(fast))
        def _():
            run(functools.partial(_dir_slow, reverse=False, row_ref=row_ref),
                functools.partial(_dir_slow, reverse=True, row_ref=row_ref))

    n_sub = qf_ref.shape[0] // c
    for sub in range(n_sub):
        chunk_step(slice(sub * c, (sub + 1) * c), slice((n_sub - 1 - sub) * c, (n_sub - sub) * c))


def _scan(q, kf, kb, v, gf, gb, *, batch, seq, heads, dk, dv, chunk):
    n = batch * seq
    c = min(chunk, seq)
    rows = min(SCAN_CHUNKS_PER_STEP * c, seq)
    nb = seq // rows
    kw, vw = heads * dk, heads * dv
    fwd = lambda blk: (lambda b, i: (b * nb + i, blk))
    bwd = lambda blk: (lambda b, i: (b * nb + (nb - 1 - i), blk))
    out_f = lambda b, i: (b * nb + i, 0)
    out_b = lambda b, i: (b * nb + (nb - 1 - i), 0)
    return pl.pallas_call(
        functools.partial(_scan_kernel, heads=heads, dk=dk, dv=dv, c=c),
        grid=(batch, nb),
        in_specs=[
            pl.BlockSpec((rows, kw), fwd(q[1])), pl.BlockSpec((rows, kw), fwd(kf[1])),
            pl.BlockSpec((rows, vw), fwd(v[1])), pl.BlockSpec((rows, kw), fwd(gf[1])),
            pl.BlockSpec((rows, kw), bwd(q[1])), pl.BlockSpec((rows, kw), bwd(kb[1])),
            pl.BlockSpec((rows, vw), bwd(v[1])), pl.BlockSpec((rows, kw), bwd(gb[1])),
        ],
        out_specs=[pl.BlockSpec((rows, vw), out_f), pl.BlockSpec((rows, vw), out_b)],
        out_shape=[jax.ShapeDtypeStruct((n, vw), BF16), jax.ShapeDtypeStruct((n, vw), BF16)],
        scratch_shapes=[
            pltpu.VMEM((heads, dv, dk), F32),
            pltpu.VMEM((heads, dv, dk), F32),
            pltpu.VMEM((2, c, dk), F32),
        ],
        compiler_params=pltpu.CompilerParams(
            dimension_semantics=("arbitrary", "arbitrary"), vmem_limit_bytes=VMEM_LIMIT),
        name=f"scan_h{heads}",
    )(q[0], kf[0], v[0], gf[0], q[0], kb[0], v[0], gb[0])


def _outproj_kernel(ofg_ref, obg_ref, ofh_ref, obh_ref, gog_ref, hog_ref, h_ref, w_ref,
                    gng_ref, hng_ref, lng_ref, lnb_ref, o_ref, y_ref):
    def head_group(of_ref, ob_ref, gate_ref, g_ref, heads, dv, base):
        for hh in range(heads):
            sl = slice(hh * dv, (hh + 1) * dv)
            o = of_ref[:, sl].astype(F32) + ob_ref[:, sl].astype(F32)
            ms = jnp.mean(o * o, axis=-1, keepdims=True)
            y = o * lax.rsqrt(ms + RMS_EPS) * g_ref[...]
            gate = gate_ref[:, sl].astype(F32)
            y = y * (gate * (1.0 / (1.0 + jnp.exp(-gate))))
            y_ref[:, base + hh * dv: base + (hh + 1) * dv] = y.astype(BF16)

    head_group(ofg_ref, obg_ref, gog_ref, gng_ref, GLA_HEADS, GLA_DV, 0)
    head_group(ofh_ref, obh_ref, hog_ref, hng_ref, HG_HEADS, HG_DIM, GLA_WIDTH)
    m = jnp.dot(y_ref[...], w_ref[...], preferred_element_type=F32)
    o_ref[...] = _layernorm(ALPHA * h_ref[...] + m, lng_ref[...], lnb_ref[...])


def _outproj(ofg, obg, ofh, obh, z, h, layer, w_out, gng, hng, lng, lnb):
    n = h.shape[0]
    tm = min(OUT_TM, n)
    half = lambda i: (i, 0)
    const = lambda i: (0, 0)
    return pl.pallas_call(
        _outproj_kernel,
        grid=(n // tm,),
        in_specs=[
            pl.BlockSpec((tm, GLA_WIDTH), half), pl.BlockSpec((tm, GLA_WIDTH), half),
            pl.BlockSpec((tm, HG_WIDTH), half), pl.BlockSpec((tm, HG_WIDTH), half),
            pl.BlockSpec((tm, GLA_WIDTH), lambda i: (i, 2048 // GLA_WIDTH)),
            pl.BlockSpec((tm, HG_WIDTH), lambda i: (i, 7168 // HG_WIDTH)),
            pl.BlockSpec((tm, D_MODEL), half),
            pl.BlockSpec((None, D_MODEL, D_MODEL), lambda i: (layer, 0, 0), pipeline_mode=pl.Buffered(1)),
            pl.BlockSpec((1, GLA_DV), const), pl.BlockSpec((1, HG_DIM), const),
            pl.BlockSpec((1, D_MODEL), const), pl.BlockSpec((1, D_MODEL), const),
        ],
        out_specs=pl.BlockSpec((tm, D_MODEL), half),
        out_shape=jax.ShapeDtypeStruct((n, D_MODEL), F32),
        scratch_shapes=[pltpu.VMEM((tm, D_MODEL), BF16)],
        compiler_params=pltpu.CompilerParams(
            dimension_semantics=("arbitrary",), vmem_limit_bytes=VMEM_LIMIT),
        name="outproj_ln1",
    )(ofg, obg, ofh, obh, z, z, h, w_out, gng, hng, lng, lnb)


def _mlp_kernel(h_ref, wup_ref, wdn_ref, lng_ref, lnb_ref, o_ref, xb_ref, a_ref):
    j = pl.program_id(1)

    @pl.when(j == 0)
    def _():
        xb_ref[...] = h_ref[...].astype(BF16)
        o_ref[...] = jnp.zeros_like(o_ref)

    for c0 in range(0, a_ref.shape[1], MXU_COLS):
        cols = slice(c0, c0 + MXU_COLS)
        a = jnp.dot(xb_ref[...], wup_ref[:, cols], preferred_element_type=F32)
        a_ref[:, cols] = jnp.square(jnp.maximum(a, 0.0)).astype(BF16)
    o_ref[...] += jnp.dot(a_ref[...], wdn_ref[...], preferred_element_type=F32)

    @pl.when(j == pl.num_programs(1) - 1)
    def _():
        for r0 in range(0, o_ref.shape[0], LN_ROWS):
            rows = slice(r0, r0 + LN_ROWS)
            o_ref[rows, :] = _layernorm(ALPHA * h_ref[rows, :] + o_ref[rows, :], lng_ref[...], lnb_ref[...])


def _mlp(h, layer, w_up, w_down, lng, lnb):
    n = h.shape[0]
    tm = min(MLP_TM, n)
    return pl.pallas_call(
        _mlp_kernel,
        grid=(n // tm, D_FF // MLP_TF),
        in_specs=[
            pl.BlockSpec((tm, D_MODEL), lambda i, j: (i, 0)),
            pl.BlockSpec((None, D_MODEL, MLP_TF), lambda i, j: (layer, 0, j)),
            pl.BlockSpec((None, MLP_TF, D_MODEL), lambda i, j: (layer, j, 0)),
            pl.BlockSpec((1, D_MODEL), lambda i, j: (0, 0)),
            pl.BlockSpec((1, D_MODEL), lambda i, j: (0, 0)),
        ],
        out_specs=pl.BlockSpec((tm, D_MODEL), lambda i, j: (i, 0)),
        out_shape=jax.ShapeDtypeStruct((n, D_MODEL), F32),
        scratch_shapes=[pltpu.VMEM((tm, D_MODEL), BF16), pltpu.VMEM((tm, MLP_TF), BF16)],
        compiler_params=pltpu.CompilerParams(
            dimension_semantics=("arbitrary", "arbitrary"), vmem_limit_bytes=MLP_VMEM_LIMIT),
        name="mlp_ln2",
    )(h, w_up, w_down, lng, lnb)


def _lower_bounds(p):
    c = jnp.cumsum(jax.nn.softmax(p.astype(F32), axis=0), axis=0)
    return c - c[0:1]


def _prep_params(w_in, gla_w_lr2, gla_b_lr, gla_norm_g, hg_norm_g, lbs_f, lbs_b,
                 w_out, ln1_g, ln1_b, w_up, w_down, ln2_g, ln2_b):
    glr0 = 2 * GLA_KWIDTH + 2 * GLA_WIDTH
    w_main = jnp.concatenate([w_in[..., :glr0], w_in[..., glr0 + 2 * GLA_RANK:]], axis=-1).astype(BF16)
    w_glr = jnp.pad(w_in[..., glr0:glr0 + 2 * GLA_RANK],
                    ((0, 0), (0, 0), (0, GLR_PAD - 2 * GLA_RANK))).astype(BF16)
    stacked = dict(w_main=w_main, w_glr=w_glr, w_out=w_out.astype(BF16),
                   w_up=w_up.astype(BF16), w_down=w_down.astype(BF16))
    layers = []
    for l in range(DEPTH):
        w2 = gla_w_lr2[l].astype(F32)
        w2c = jnp.zeros((GLR_PAD, 2 * GLA_KWIDTH), F32)
        w2c = w2c.at[:GLA_RANK, :GLA_KWIDTH].set(w2[0]).at[GLA_RANK:2 * GLA_RANK, GLA_KWIDTH:].set(w2[1])
        lb = jnp.concatenate([lbs_f[l], lbs_b[l]])[None, :]
        row = lambda a: a[l].astype(F32)[None, :]
        layers.append(dict(
            w2c=w2c.astype(BF16), b2c=gla_b_lr[l].astype(F32).reshape(1, 2 * GLA_KWIDTH),
            gate_c=jnp.concatenate([lb, 1.0 - lb, jnp.log(lb), jnp.log1p(-lb),
                                    jnp.zeros((4, 2 * HG_WIDTH), F32)], axis=0),
            gng=row(gla_norm_g), hng=row(hg_norm_g), ln1_g=row(ln1_g), ln1_b=row(ln1_b),
            ln2_g=row(ln2_g), ln2_b=row(ln2_b)))
    return stacked, layers


def _trunk(x, stacked, layers):
    batch, seq, _ = x.shape
    h = x.reshape(batch * seq, D_MODEL).astype(F32)
    for l, p in enumerate(layers):
        z, logf, ga = _inproj(h, l, stacked["w_main"], stacked["w_glr"], p["w2c"], p["b2c"], p["gate_c"])
        ofg, obg = _scan((z, 0), (z, 1), (z, 1), (z, 1), (ga, 0), (ga, 1),
                         batch=batch, seq=seq, heads=GLA_HEADS, dk=GLA_DK, dv=GLA_DV, chunk=GLA_CHUNK)
        ofh, obh = _scan((z, 3), (z, 4), (z, 5), (z, 6), (logf, 0), (logf, 1),
                         batch=batch, seq=seq, heads=HG_HEADS, dk=HG_DIM, dv=HG_DIM, chunk=HG_CHUNK)
        h = _outproj(ofg, obg, ofh, obh, z, h, l, stacked["w_out"], p["gng"], p["hng"], p["ln1_g"], p["ln1_b"])
        h = _mlp(h, l, stacked["w_up"], stacked["w_down"], p["ln2_g"], p["ln2_b"])
    return h.reshape(batch, seq, D_MODEL).astype(x.dtype)


def kernel(x_prompt, x_sample, w_in, gla_w_lr2, gla_b_lr, gla_norm_g, hg_norm_g, lower_bounds,
           w_out, ln1_g, ln1_b, w_up, w_down, ln2_g, ln2_b):
    lbs_f = _lower_bounds(lower_bounds[0])
    lbs_b = _lower_bounds(lower_bounds[1])
    stacked, layers = _prep_params(w_in, gla_w_lr2, gla_b_lr, gla_norm_g, hg_norm_g, lbs_f, lbs_b,
                                   w_out, ln1_g, ln1_b, w_up, w_down, ln2_g, ln2_b)
    return (_trunk(x_prompt, stacked, layers), _trunk(x_sample, stacked, layers))
```

```python
import functools

import jax
import jax.numpy as jnp
from jax import lax
from jax.experimental import pallas as pl
from jax.experimental.pallas import tpu as pltpu

F32 = jnp.float32
BF16 = jnp.bfloat16

D_MODEL = 2048
DEPTH = 2
GLA_HEADS = 4
GLA_WIDTH = D_MODEL // 2
GLA_DV = GLA_WIDTH // GLA_HEADS
GLA_DK = GLA_DV // 2
GLA_KWIDTH = GLA_HEADS * GLA_DK
GLA_RANK = 16
GLA_TAU = 16.0
HG_DIM = 128
HG_WIDTH = D_MODEL - GLA_WIDTH
HG_HEADS = HG_WIDTH // HG_DIM
D_FF = 4 * D_MODEL
LN_EPS = 1e-5
RMS_EPS = 1e-6
F32_TINY = 1.1754943508222875e-38
ALPHA = (2.0 * DEPTH) ** 0.25

LANES = 128
GLR_PAD = LANES
Z_WIDTH = 8192
VMEM_LIMIT = 48 * 1024 * 1024
IN_VMEM_LIMIT = 58 * 1024 * 1024

IN_TN = 1024
IN_TM = 1024
MXU_COLS = 256
GATE_TILE_LO = 4096 // IN_TN
GATE_TILE_HI = 6144 // IN_TN

GLA_CHUNK = 256
HG_CHUNK = 128
SCAN_CHUNKS_PER_STEP = 4
MAX_HALF_RANGE = 80.0

OUT_TM = 512
MLP_TM = 1024
MLP_TF = 1024
LN_ROWS = 128
MLP_VMEM_LIMIT = 60 * 1024 * 1024

_NT = (((1,), (1,)), ((), ()))
_TN = (((0,), (0,)), ((), ()))


def _log_sigmoid(x):
    return jnp.minimum(x, 0.0) - jnp.log(1.0 + jnp.exp(-jnp.abs(x)))


def _layernorm(r, g, b):
    mu = jnp.mean(r, axis=-1, keepdims=True)
    d = r - mu
    var = jnp.mean(d * d, axis=-1, keepdims=True)
    return d * lax.rsqrt(var + LN_EPS) * g + b


def _inproj_kernel(x_ref, w_ref, wglr_ref, w2_ref, b2_ref, gc_ref, z_ref, logf_ref, ga_ref, xb_ref):
    j = pl.program_id(1)
    is_gate = jnp.logical_and(j >= GATE_TILE_LO, j < GATE_TILE_HI)
    col_groups = lambda width: [slice(c, c + MXU_COLS) for c in range(0, width, MXU_COLS)]

    last = pl.num_programs(1) - 1

    def project_plain():
        for cols in col_groups(z_ref.shape[1]):
            z_ref[:, cols] = jnp.dot(xb_ref[...], w_ref[:, cols], preferred_element_type=F32).astype(BF16)

    @pl.when(j == 0)
    def _():
        xb = x_ref[...].astype(BF16)
        xb_ref[...] = xb
        for cols in col_groups(z_ref.shape[1]):
            zc = jnp.dot(xb, w_ref[:, cols], preferred_element_type=F32)
            if cols.stop <= GLA_KWIDTH:
                zc = zc * (GLA_DK ** -0.5)
            z_ref[:, cols] = zc.astype(BF16)

    @pl.when(jnp.logical_and(jnp.logical_and(j > 0, j < last), jnp.logical_not(is_gate)))
    def _():
        project_plain()

    @pl.when(j == last)
    def _():
        project_plain()
        glr = jnp.dot(xb_ref[...], wglr_ref[...], preferred_element_type=F32).astype(BF16)
        for cols in col_groups(ga_ref.shape[1]):
            a = jnp.dot(glr, w2_ref[:, cols], preferred_element_type=F32) + b2_ref[:, cols]
            ga_ref[:, cols] = _log_sigmoid(a) * (1.0 / GLA_TAU)

    @pl.when(is_gate)
    def _():
        f_min = None
        for cols in col_groups(z_ref.shape[1]):
            lb = gc_ref[0:1, cols]
            one_minus_lb = gc_ref[1:2, cols]
            acc = jnp.dot(xb_ref[...], w_ref[:, cols], preferred_element_type=F32)
            e = jnp.exp(-jnp.abs(acc))
            r = 1.0 / (1.0 + e)
            er = e * r
            pos = acc >= 0.0
            f = lb + one_minus_lb * jnp.where(pos, r, er)
            logf_ref[:, cols] = jnp.log(f)
            z_ref[:, cols] = (one_minus_lb * jnp.where(pos, er, r)).astype(BF16)
            f_min = f if f_min is None else jnp.minimum(f_min, f)

        @pl.when(jnp.logical_not(jnp.min(f_min) >= F32_TINY))
        def _():
            acc = jnp.dot(xb_ref[...], w_ref[...], preferred_element_type=F32)
            log_lb = gc_ref[2:3, :]
            x2 = gc_ref[3:4, :] + _log_sigmoid(acc)
            logf_ref[...] = (jnp.maximum(log_lb, x2)
                             + jnp.log(1.0 + jnp.exp(-jnp.abs(log_lb - x2))))


def _inproj(x, layer, w_main, w_glr, w2c, b2c, gate_c):
    n = x.shape[0]
    tm = min(IN_TM, n)
    n_gate = GATE_TILE_HI - GATE_TILE_LO
    gate_idx = lambda i, j: jnp.clip(j - GATE_TILE_LO, 0, n_gate - 1)
    return pl.pallas_call(
        _inproj_kernel,
        grid=(n // tm, Z_WIDTH // IN_TN),
        in_specs=[
            pl.BlockSpec((tm, D_MODEL), lambda i, j: (i, 0)),
            pl.BlockSpec((None, D_MODEL, IN_TN), lambda i, j: (layer, 0, j)),
            pl.BlockSpec((None, D_MODEL, GLR_PAD), lambda i, j: (layer, 0, 0)),
            pl.BlockSpec((GLR_PAD, 2 * GLA_KWIDTH), lambda i, j: (0, 0)),
            pl.BlockSpec((1, 2 * GLA_KWIDTH), lambda i, j: (0, 0)),
            pl.BlockSpec((8, IN_TN), lambda i, j: (0, gate_idx(i, j))),
        ],
        out_specs=[
            pl.BlockSpec((tm, IN_TN), lambda i, j: (i, j)),
            pl.BlockSpec((tm, IN_TN), lambda i, j: (i, gate_idx(i, j))),
            pl.BlockSpec((tm, 2 * GLA_KWIDTH), lambda i, j: (i, 0)),
        ],
        out_shape=[
            jax.ShapeDtypeStruct((n, Z_WIDTH), BF16),
            jax.ShapeDtypeStruct((n, 2 * HG_WIDTH), F32),
            jax.ShapeDtypeStruct((n, 2 * GLA_KWIDTH), F32),
        ],
        scratch_shapes=[pltpu.VMEM((tm, D_MODEL), BF16)],
        compiler_params=pltpu.CompilerParams(
            dimension_semantics=("arbitrary", "arbitrary"), vmem_limit_bytes=IN_VMEM_LIMIT),
        name="inproj",
    )(x, w_main, w_glr, w2c, b2c, gate_c)


def _cumsum_rows(tri, g):
    hi = g.astype(BF16)
    lo = (g - hi.astype(F32)).astype(BF16)
    return jnp.dot(jnp.concatenate([tri, tri], axis=1), jnp.concatenate([hi, lo], axis=0),
                   preferred_element_type=F32)


def _dir_fast(q, k, v, b, tot, st, mask):
    m = 0.5 * tot
    qt = q * jnp.exp(b - m).astype(BF16)
    kt = k * jnp.exp(m - b).astype(BF16)
    c = q.shape[0]
    ssc = (st * jnp.exp(m)).astype(BF16)
    r = lax.dot_general(qt, jnp.concatenate([kt, ssc], axis=0), _NT, preferred_element_type=F32)
    s = jnp.where(mask, r[:, :c], 0.0).astype(BF16)
    o = jnp.dot(s, v, preferred_element_type=F32) + r[:, c:]
    u = lax.dot_general(v, kt, _TN, preferred_element_type=F32)
    return o, st * jnp.exp(tot) + u * jnp.exp(tot - m)


def _dir_slow(q, k, v, b, tot, st, reverse, row_ref):
    c = q.shape[0]
    q = q.astype(F32)
    k = k.astype(F32)
    row_ref[0] = b
    row_ref[1] = q
    jj = lax.broadcasted_iota(jnp.int32, (c, 1), 0)
    lane = lax.broadcasted_iota(jnp.int32, (c, c), 1)

    def body(i, s_t):
        bi = row_ref[0, pl.ds(i, 1), :]
        qi = row_ref[1, pl.ds(i, 1), :]
        valid = (jj >= i) if reverse else (jj <= i)
        w = jnp.where(valid, jnp.exp(jnp.minimum(bi - b, 0.0)), 0.0)
        col = jnp.sum(qi * k * w, axis=-1, keepdims=True)
        return jnp.where(lane == i, col, s_t)

    s_t = lax.fori_loop(0, c, body, jnp.zeros((c, c), F32))
    o = lax.dot_general(s_t.astype(BF16), v, _TN, preferred_element_type=F32)
    qin = (q * jnp.exp(b)).astype(BF16)
    o = o + lax.dot_general(qin, st.astype(BF16), _NT, preferred_element_type=F32)
    kout = (k * jnp.exp(tot - b)).astype(BF16)
    u = lax.dot_general(v, kout, _TN, preferred_element_type=F32)
    return o, st * jnp.exp(tot) + u


def _scan_kernel(qf_ref, kf_ref, vf_ref, gf_ref, qb_ref, kb_ref, vb_ref, gb_ref,
                 of_ref, ob_ref, sf_ref, sb_ref, row_ref, *, heads, dk, dv, c):
    @pl.when(pl.program_id(1) == 0)
    def _():
        sf_ref[...] = jnp.zeros_like(sf_ref)
        sb_ref[...] = jnp.zeros_like(sb_ref)

    row = lax.broadcasted_iota(jnp.int32, (c, c), 0)
    col = lax.broadcasted_iota(jnp.int32, (c, c), 1)
    lower = row >= col
    upper = row <= col

    def chunk_step(rf, rb):
        bf_all = _cumsum_rows(jnp.where(lower, 1.0, 0.0).astype(BF16), gf_ref[rf, :])
        bb_all = _cumsum_rows(jnp.where(upper, 1.0, 0.0).astype(BF16), gb_ref[rb, :])
        tot_f = bf_all[c - 1:c, :]
        tot_b = bb_all[0:1, :]
        span = jnp.maximum(jnp.max(-tot_f), jnp.max(-tot_b))
        fast = span < 2.0 * MAX_HALF_RANGE

        def run(step_f, step_b):
            for h in range(heads):
                ks = slice(h * dk, (h + 1) * dk)
                vs = slice(h * dv, (h + 1) * dv)
                o, s_new = step_f(qf_ref[rf, ks], kf_ref[rf, ks], vf_ref[rf, vs],
                                  bf_all[:, ks], tot_f[:, ks], sf_ref[h])
                of_ref[rf, vs] = o.astype(BF16)
                sf_ref[h] = s_new
                o, s_new = step_b(qb_ref[rb, ks], kb_ref[rb, ks], vb_ref[rb, vs],
                                  bb_all[:, ks], tot_b[:, ks], sb_ref[h])
                ob_ref[rb, vs] = o.astype(BF16)
                sb_ref[h] = s_new

        @pl.when(fast)
        def _():
            run(functools.partial(_dir_fast, mask=lower), functools.partial(_dir_fast, mask=upper))

        @pl.when(jnp.logical_not(fast))
        def _():
            run(functools.partial(_dir_slow, reverse=False, row_ref=row_ref),
                functools.partial(_dir_slow, reverse=True, row_ref=row_ref))

    n_sub = qf_ref.shape[0] // c
    for sub in range(n_sub):
        chunk_step(slice(sub * c, (sub + 1) * c), slice((n_sub - 1 - sub) * c, (n_sub - sub) * c))


def _scan(q, kf, kb, v, gf, gb, *, batch, seq, heads, dk, dv, chunk):
    n = batch * seq
    c = min(chunk, seq)
    rows = min(SCAN_CHUNKS_PER_STEP * c, seq)
    nb = seq // rows
    kw, vw = heads * dk, heads * dv
    fwd = lambda blk: (lambda b, i: (b * nb + i, blk))
    bwd = lambda blk: (lambda b, i: (b * nb + (nb - 1 - i), blk))
    out_f = lambda b, i: (b * nb + i, 0)
    out_b = lambda b, i: (b * nb + (nb - 1 - i), 0)
    return pl.pallas_call(
        functools.partial(_scan_kernel, heads=heads, dk=dk, dv=dv, c=c),
        grid=(batch, nb),
        in_specs=[
            pl.BlockSpec((rows, kw), fwd(q[1])), pl.BlockSpec((rows, kw), fwd(kf[1])),
            pl.BlockSpec((rows, vw), fwd(v[1])), pl.BlockSpec((rows, kw), fwd(gf[1])),
            pl.BlockSpec((rows, kw), bwd(q[1])), pl.BlockSpec((rows, kw), bwd(kb[1])),
            pl.BlockSpec((rows, vw), bwd(v[1])), pl.BlockSpec((rows, kw), bwd(gb[1])),
        ],
        out_specs=[pl.BlockSpec((rows, vw), out_f), pl.BlockSpec((rows, vw), out_b)],
        out_shape=[jax.ShapeDtypeStruct((n, vw), BF16), jax.ShapeDtypeStruct((n, vw), BF16)],
        scratch_shapes=[
            pltpu.VMEM((heads, dv, dk), F32),
            pltpu.VMEM((heads, dv, dk), F32),
            pltpu.VMEM((2, c, dk), F32),
        ],
        compiler_params=pltpu.CompilerParams(
            dimension_semantics=("arbitrary", "arbitrary"), vmem_limit_bytes=VMEM_LIMIT),
        name=f"scan_h{heads}",
    )(q[0], kf[0], v[0], gf[0], q[0], kb[0], v[0], gb[0])


def _outproj_kernel(ofg_ref, obg_ref, ofh_ref, obh_ref, gog_ref, hog_ref, h_ref, w_ref,
                    gng_ref, hng_ref, lng_ref, lnb_ref, o_ref, y_ref):
    def head_group(of_ref, ob_ref, gate_ref, g_ref, heads, dv, base):
        for hh in range(heads):
            sl = slice(hh * dv, (hh + 1) * dv)
            o = of_ref[:, sl].astype(F32) + ob_ref[:, sl].astype(F32)
            ms = jnp.mean(o * o, axis=-1, keepdims=True)
            y = o * lax.rsqrt(ms + RMS_EPS) * g_ref[...]
            gate = gate_ref[:, sl].astype(F32)
            y = y * (gate * (1.0 / (1.0 + jnp.exp(-gate))))
            y_ref[:, base + hh * dv: base + (hh + 1) * dv] = y.astype(BF16)

    head_group(ofg_ref, obg_ref, gog_ref, gng_ref, GLA_HEADS, GLA_DV, 0)
    head_group(ofh_ref, obh_ref, hog_ref, hng_ref, HG_HEADS, HG_DIM, GLA_WIDTH)
    m = jnp.dot(y_ref[...], w_ref[...], preferred_element_type=F32)
    o_ref[...] = _layernorm(ALPHA * h_ref[...] + m, lng_ref[...], lnb_ref[...])


def _outproj(ofg, obg, ofh, obh, z, h, layer, w_out, gng, hng, lng, lnb):
    n = h.shape[0]
    tm = min(OUT_TM, n)
    half = lambda i: (i, 0)
    const = lambda i: (0, 0)
    return pl.pallas_call(
        _outproj_kernel,
        grid=(n // tm,),
        in_specs=[
            pl.BlockSpec((tm, GLA_WIDTH), half), pl.BlockSpec((tm, GLA_WIDTH), half),
            pl.BlockSpec((tm, HG_WIDTH), half), pl.BlockSpec((tm, HG_WIDTH), half),
            pl.BlockSpec((tm, GLA_WIDTH), lambda i: (i, 2048 // GLA_WIDTH)),
            pl.BlockSpec((tm, HG_WIDTH), lambda i: (i, 7168 // HG_WIDTH)),
            pl.BlockSpec((tm, D_MODEL), half),
            pl.BlockSpec((None, D_MODEL, D_MODEL), lambda i: (layer, 0, 0), pipeline_mode=pl.Buffered(1)),
            pl.BlockSpec((1, GLA_DV), const), pl.BlockSpec((1, HG_DIM), const),
            pl.BlockSpec((1, D_MODEL), const), pl.BlockSpec((1, D_MODEL), const),
        ],
        out_specs=pl.BlockSpec((tm, D_MODEL), half),
        out_shape=jax.ShapeDtypeStruct((n, D_MODEL), F32),
        scratch_shapes=[pltpu.VMEM((tm, D_MODEL), BF16)],
        compiler_params=pltpu.CompilerParams(
            dimension_semantics=("arbitrary",), vmem_limit_bytes=VMEM_LIMIT),
        name="outproj_ln1",
    )(ofg, obg, ofh, obh, z, z, h, w_out, gng, hng, lng, lnb)


def _mlp_kernel(h_ref, wup_ref, wdn_ref, lng_ref, lnb_ref, o_ref, xb_ref, a_ref):
    j = pl.program_id(1)

    def ff_tile():
        for c0 in range(0, a_ref.shape[1], MXU_COLS):
            cols = slice(c0, c0 + MXU_COLS)
            a = jnp.dot(xb_ref[...], wup_ref[:, cols], preferred_element_type=F32)
            a_ref[:, cols] = jnp.square(jnp.maximum(a, 0.0)).astype(BF16)
        return jnp.dot(a_ref[...], wdn_ref[...], preferred_element_type=F32)

    @pl.when(j == 0)
    def _():
        xb_ref[...] = h_ref[...].astype(BF16)
        o_ref[...] = ff_tile()

    @pl.when(j > 0)
    def _():
        o_ref[...] += ff_tile()

    @pl.when(j == pl.num_programs(1) - 1)
    def _():
        for r0 in range(0, o_ref.shape[0], LN_ROWS):
            rows = slice(r0, r0 + LN_ROWS)
            o_ref[rows, :] = _layernorm(ALPHA * h_ref[rows, :] + o_ref[rows, :], lng_ref[...], lnb_ref[...])


def _mlp(h, layer, w_up, w_down, lng, lnb):
    n = h.shape[0]
    tm = min(MLP_TM, n)
    return pl.pallas_call(
        _mlp_kernel,
        grid=(n // tm, D_FF // MLP_TF),
        in_specs=[
            pl.BlockSpec((tm, D_MODEL), lambda i, j: (i, 0)),
            pl.BlockSpec((None, D_MODEL, MLP_TF), lambda i, j: (layer, 0, j)),
            pl.BlockSpec((None, MLP_TF, D_MODEL), lambda i, j: (layer, j, 0)),
            pl.BlockSpec((1, D_MODEL), lambda i, j: (0, 0)),
            pl.BlockSpec((1, D_MODEL), lambda i, j: (0, 0)),
        ],
        out_specs=pl.BlockSpec((tm, D_MODEL), lambda i, j: (i, 0)),
        out_shape=jax.ShapeDtypeStruct((n, D_MODEL), F32),
        scratch_shapes=[pltpu.VMEM((tm, D_MODEL), BF16), pltpu.VMEM((tm, MLP_TF), BF16)],
        compiler_params=pltpu.CompilerParams(
            dimension_semantics=("arbitrary", "arbitrary"), vmem_limit_bytes=MLP_VMEM_LIMIT),
        name="mlp_ln2",
    )(h, w_up, w_down, lng, lnb)


def _lower_bounds(p):
    c = jnp.cumsum(jax.nn.softmax(p.astype(F32), axis=0), axis=0)
    return c - c[0:1]


def _prep_params(w_in, gla_w_lr2, gla_b_lr, gla_norm_g, hg_norm_g, lbs_f, lbs_b,
                 w_out, ln1_g, ln1_b, w_up, w_down, ln2_g, ln2_b):
    glr0 = 2 * GLA_KWIDTH + 2 * GLA_WIDTH
    w_main = jnp.concatenate([w_in[..., :glr0], w_in[..., glr0 + 2 * GLA_RANK:]], axis=-1).astype(BF16)
    w_glr = jnp.pad(w_in[..., glr0:glr0 + 2 * GLA_RANK],
                    ((0, 0), (0, 0), (0, GLR_PAD - 2 * GLA_RANK))).astype(BF16)
    stacked = dict(w_main=w_main, w_glr=w_glr, w_out=w_out.astype(BF16),
                   w_up=w_up.astype(BF16), w_down=w_down.astype(BF16))
    layers = []
    for l in range(DEPTH):
        w2 = gla_w_lr2[l].astype(F32)
        w2c = jnp.zeros((GLR_PAD, 2 * GLA_KWIDTH), F32)
        w2c = w2c.at[:GLA_RANK, :GLA_KWIDTH].set(w2[0]).at[GLA_RANK:2 * GLA_RANK, GLA_KWIDTH:].set(w2[1])
        lb = jnp.concatenate([lbs_f[l], lbs_b[l]])[None, :]
        row = lambda a: a[l].astype(F32)[None, :]
        layers.append(dict(
            w2c=w2c.astype(BF16), b2c=gla_b_lr[l].astype(F32).reshape(1, 2 * GLA_KWIDTH),
            gate_c=jnp.concatenate([lb, 1.0 - lb, jnp.log(lb), jnp.log1p(-lb),
                                    jnp.zeros((4, 2 * HG_WIDTH), F32)], axis=0),
            gng=row(gla_norm_g), hng=row(hg_norm_g), ln1_g=row(ln1_g), ln1_b=row(ln1_b),
            ln2_g=row(ln2_g), ln2_b=row(ln2_b)))
    return stacked, layers


def _trunk(x, stacked, layers):
    batch, seq, _ = x.shape
    h = x.reshape(batch * seq, D_MODEL).astype(F32)
    for l, p in enumerate(layers):
        z, logf, ga = _inproj(h, l, stacked["w_main"], stacked["w_glr"], p["w2c"], p["b2c"], p["gate_c"])
        ofg, obg = _scan((z, 0), (z, 1), (z, 1), (z, 1), (ga, 0), (ga, 1),
                         batch=batch, seq=seq, heads=GLA_HEADS, dk=GLA_DK, dv=GLA_DV, chunk=GLA_CHUNK)
        ofh, obh = _scan((z, 3), (z, 4), (z, 5), (z, 6), (logf, 0), (logf, 1),
                         batch=batch, seq=seq, heads=HG_HEADS, dk=HG_DIM, dv=HG_DIM, chunk=HG_CHUNK)
        h = _outproj(ofg, obg, ofh, obh, z, h, l, stacked["w_out"], p["gng"], p["hng"], p["ln1_g"], p["ln1_b"])
        h = _mlp(h, l, stacked["w_up"], stacked["w_down"], p["ln2_g"], p["ln2_b"])
    return h.reshape(batch, seq, D_MODEL).astype(x.dtype)


def kernel(x_prompt, x_sample, w_in, gla_w_lr2, gla_b_lr, gla_norm_g, hg_norm_g, lower_bounds,
           w_out, ln1_g, ln1_b, w_up, w_down, ln2_g, ln2_b):
    lbs_f = _lower_bounds(lower_bounds[0])
    lbs_b = _lower_bounds(lower_bounds[1])
    stacked, layers = _prep_params(w_in, gla_w_lr2, gla_b_lr, gla_norm_g, hg_norm_g, lbs_f, lbs_b,
                                   w_out, ln1_g, ln1_b, w_up, w_down, ln2_g, ln2_b)
    return (_trunk(x_prompt, stacked, layers), _trunk(x_sample, stacked, layers))
```

```python
import functools

import jax
import jax.numpy as jnp
from jax import lax
from jax.experimental import pallas as pl
from jax.experimental.pallas import tpu as pltpu

F32 = jnp.float32
BF16 = jnp.bfloat16

D_MODEL = 2048
DEPTH = 2
GLA_HEADS = 4
GLA_WIDTH = D_MODEL // 2
GLA_DV = GLA_WIDTH // GLA_HEADS
GLA_DK = GLA_DV // 2
GLA_KWIDTH = GLA_HEADS * GLA_DK
GLA_RANK = 16
GLA_TAU = 16.0
HG_DIM = 128
HG_WIDTH = D_MODEL - GLA_WIDTH
HG_HEADS = HG_WIDTH // HG_DIM
D_FF = 4 * D_MODEL
LN_EPS = 1e-5
RMS_EPS = 1e-6
F32_TINY = 1.1754943508222875e-38
ALPHA = (2.0 * DEPTH) ** 0.25

LANES = 128
GLR_PAD = LANES
Z_WIDTH = 8192
VMEM_LIMIT = 48 * 1024 * 1024
IN_VMEM_LIMIT = 58 * 1024 * 1024

IN_TN = 1024
IN_TM = 1024
MXU_COLS = 256
GATE_TILE_LO = 4096 // IN_TN
GATE_TILE_HI = 6144 // IN_TN

GLA_CHUNK = 256
HG_CHUNK = 128
SCAN_CHUNKS_PER_STEP = 4
MAX_HALF_RANGE = 80.0

OUT_TM = 512
MLP_TM = 1024
MLP_TF = 1024
LN_ROWS = 128
MLP_VMEM_LIMIT = 60 * 1024 * 1024

_NT = (((1,), (1,)), ((), ()))
_TN = (((0,), (0,)), ((), ()))


def _log_sigmoid(x):
    return jnp.minimum(x, 0.0) - jnp.log(1.0 + jnp.exp(-jnp.abs(x)))


def _layernorm(r, g, b):
    mu = jnp.mean(r, axis=-1, keepdims=True)
    d = r - mu
    var = jnp.mean(d * d, axis=-1, keepdims=True)
    return d * lax.rsqrt(var + LN_EPS) * g + b


def _inproj_kernel(x_ref, w_ref, wglr_ref, w2_ref, b2_ref, gc_ref, z_ref, logf_ref, ga_ref, xb_ref):
    j = pl.program_id(1)
    is_gate = jnp.logical_and(j >= GATE_TILE_LO, j < GATE_TILE_HI)
    col_groups = lambda width: [slice(c, c + MXU_COLS) for c in range(0, width, MXU_COLS)]

    last = pl.num_programs(1) - 1

    def project_plain():
        for cols in col_groups(z_ref.shape[1]):
            z_ref[:, cols] = jnp.dot(xb_ref[...], w_ref[:, cols], preferred_element_type=F32).astype(BF16)

    @pl.when(j == 0)
    def _():
        xb = x_ref[...].astype(BF16)
        xb_ref[...] = xb
        for cols in col_groups(z_ref.shape[1]):
            zc = jnp.dot(xb, w_ref[:, cols], preferred_element_type=F32)
            if cols.stop <= GLA_KWIDTH:
                zc = zc * (GLA_DK ** -0.5)
            z_ref[:, cols] = zc.astype(BF16)

    @pl.when(jnp.logical_and(jnp.logical_and(j > 0, j < last), jnp.logical_not(is_gate)))
    def _():
        project_plain()

    @pl.when(j == last)
    def _():
        project_plain()
        glr = jnp.dot(xb_ref[...], wglr_ref[...], preferred_element_type=F32).astype(BF16)
        for cols in col_groups(ga_ref.shape[1]):
            a = jnp.dot(glr, w2_ref[:, cols], preferred_element_type=F32) + b2_ref[:, cols]
            ga_ref[:, cols] = _log_sigmoid(a) * (1.0 / GLA_TAU)

    @pl.when(is_gate)
    def _():
        f_min = None
        for cols in col_groups(z_ref.shape[1]):
            lb = gc_ref[0:1, cols]
            one_minus_lb = gc_ref[1:2, cols]
            acc = jnp.dot(xb_ref[...], w_ref[:, cols], preferred_element_type=F32)
            e = jnp.exp(-jnp.abs(acc))
            r = 1.0 / (1.0 + e)
            er = e * r
            pos = acc >= 0.0
            f = lb + one_minus_lb * jnp.where(pos, r, er)
            logf_ref[:, cols] = jnp.log(f)
            z_ref[:, cols] = (one_minus_lb * jnp.where(pos, er, r)).astype(BF16)
            f_min = f if f_min is None else jnp.minimum(f_min, f)

        @pl.when(jnp.logical_not(jnp.min(f_min) >= F32_TINY))
        def _():
            acc = jnp.dot(xb_ref[...], w_ref[...], preferred_element_type=F32)
            log_lb = gc_ref[2:3, :]
            x2 = gc_ref[3:4, :] + _log_sigmoid(acc)
            logf_ref[...] = (jnp.maximum(log_lb, x2)
                             + jnp.log(1.0 + jnp.exp(-jnp.abs(log_lb - x2))))


def _inproj(x, layer, w_main, w_glr, w2c, b2c, gate_c):
    n = x.shape[0]
    tm = min(IN_TM, n)
    n_gate = GATE_TILE_HI - GATE_TILE_LO
    gate_idx = lambda i, j: jnp.clip(j - GATE_TILE_LO, 0, n_gate - 1)
    return pl.pallas_call(
        _inproj_kernel,
        grid=(n // tm, Z_WIDTH // IN_TN),
        in_specs=[
            pl.BlockSpec((tm, D_MODEL), lambda i, j: (i, 0)),
            pl.BlockSpec((None, D_MODEL, IN_TN), lambda i, j: (layer, 0, j)),
            pl.BlockSpec((None, D_MODEL, GLR_PAD), lambda i, j: (layer, 0, 0)),
            pl.BlockSpec((GLR_PAD, 2 * GLA_KWIDTH), lambda i, j: (0, 0)),
            pl.BlockSpec((1, 2 * GLA_KWIDTH), lambda i, j: (0, 0)),
            pl.BlockSpec((8, IN_TN), lambda i, j: (0, gate_idx(i, j))),
        ],
        out_specs=[
            pl.BlockSpec((tm, IN_TN), lambda i, j: (i, j)),
            pl.BlockSpec((tm, IN_TN), lambda i, j: (i, gate_idx(i, j))),
            pl.BlockSpec((tm, 2 * GLA_KWIDTH), lambda i, j: (i, 0)),
        ],
        out_shape=[
            jax.ShapeDtypeStruct((n, Z_WIDTH), BF16),
            jax.ShapeDtypeStruct((n, 2 * HG_WIDTH), F32),
            jax.ShapeDtypeStruct((n, 2 * GLA_KWIDTH), F32),
        ],
        scratch_shapes=[pltpu.VMEM((tm, D_MODEL), BF16)],
        compiler_params=pltpu.CompilerParams(
            dimension_semantics=("arbitrary", "arbitrary"), vmem_limit_bytes=IN_VMEM_LIMIT),
        name="inproj",
    )(x, w_main, w_glr, w2c, b2c, gate_c)


def _cumsum_rows(tri, g):
    hi = g.astype(BF16)
    lo = (g - hi.astype(F32)).astype(BF16)
    return jnp.dot(jnp.concatenate([tri, tri], axis=1), jnp.concatenate([hi, lo], axis=0),
                   preferred_element_type=F32)


def _dir_fast(q, k, v, b, tot, st, mask):
    m = 0.5 * tot
    qt = q * jnp.exp(b - m).astype(BF16)
    kt = k * jnp.exp(m - b).astype(BF16)
    c = q.shape[0]
    ssc = (st * jnp.exp(m)).astype(BF16)
    r = lax.dot_general(qt, jnp.concatenate([kt, ssc], axis=0), _NT, preferred_element_type=F32)
    s = jnp.where(mask, r[:, :c], 0.0).astype(BF16)
    o = jnp.dot(s, v, preferred_element_type=F32) + r[:, c:]
    u = lax.dot_general(v, kt, _TN, preferred_element_type=F32)
    return o, st * jnp.exp(tot) + u * jnp.exp(tot - m)


def _dir_slow(q, k, v, b, tot, st, reverse, row_ref):
    c = q.shape[0]
    q = q.astype(F32)
    k = k.astype(F32)
    row_ref[0] = b
    row_ref[1] = q
    jj = lax.broadcasted_iota(jnp.int32, (c, 1), 0)
    lane = lax.broadcasted_iota(jnp.int32, (c, c), 1)

    def body(i, s_t):
        bi = row_ref[0, pl.ds(i, 1), :]
        qi = row_ref[1, pl.ds(i, 1), :]
        valid = (jj >= i) if reverse else (jj <= i)
        w = jnp.where(valid, jnp.exp(jnp.minimum(bi - b, 0.0)), 0.0)
        col = jnp.sum(qi * k * w, axis=-1, keepdims=True)
        return jnp.where(lane == i, col, s_t)

    s_t = lax.fori_loop(0, c, body, jnp.zeros((c, c), F32))
    o = lax.dot_general(s_t.astype(BF16), v, _TN, preferred_element_type=F32)
    qin = (q * jnp.exp(b)).astype(BF16)
    o = o + lax.dot_general(qin, st.astype(BF16), _NT, preferred_element_type=F32)
    kout = (k * jnp.exp(tot - b)).astype(BF16)
    u = lax.dot_general(v, kout, _TN, preferred_element_type=F32)
    return o, st * jnp.exp(tot) + u


def _scan_kernel(qf_ref, kf_ref, vf_ref, gf_ref, qb_ref, kb_ref, vb_ref, gb_ref,
                 of_ref, ob_ref, sf_ref, sb_ref, row_ref, *, heads, dk, dv, c):
    @pl.when(pl.program_id(1) == 0)
    def _():
        sf_ref[...] = jnp.zeros_like(sf_ref)
        sb_ref[...] = jnp.zeros_like(sb_ref)

    row = lax.broadcasted_iota(jnp.int32, (c, c), 0)
    col = lax.broadcasted_iota(jnp.int32, (c, c), 1)
    lower = row >= col
    upper = row <= col

    def chunk_step(rf, rb):
        bf_all = _cumsum_rows(jnp.where(lower, 1.0, 0.0).astype(BF16), gf_ref[rf, :])
        bb_all = _cumsum_rows(jnp.where(upper, 1.0, 0.0).astype(BF16), gb_ref[rb, :])
        tot_f = bf_all[c - 1:c, :]
        tot_b = bb_all[0:1, :]
        span = jnp.maximum(jnp.max(-tot_f), jnp.max(-tot_b))
        fast = span < 2.0 * MAX_HALF_RANGE

        def run(step_f, step_b):
            for h in range(heads):
                ks = slice(h * dk, (h + 1) * dk)
                vs = slice(h * dv, (h + 1) * dv)
                o, s_new = step_f(qf_ref[rf, ks], kf_ref[rf, ks], vf_ref[rf, vs],
                                  bf_all[:, ks], tot_f[:, ks], sf_ref[h])
                of_ref[rf, vs] = o.astype(BF16)
                sf_ref[h] = s_new
                o, s_new = step_b(qb_ref[rb, ks], kb_ref[rb, ks], vb_ref[rb, vs],
                                  bb_all[:, ks], tot_b[:, ks], sb_ref[h])
                ob_ref[rb, vs] = o.astype(BF16)
                sb_ref[h] = s_new

        @pl.when(fast)
        def _():
            run(functools.partial(_dir_fast, mask=lower), functools.partial(_dir_fast, mask=upper))

        @pl.when(jnp.logical_not(fast))
        def _():
            run(functools.partial(_dir_slow, reverse=False, row_ref=row_ref),
                functools.partial(_dir_slow, reverse=True, row_ref=row_ref))

    n_sub = qf_ref.shape[0] // c
    for sub in range(n_sub):
        chunk_step(slice(sub * c, (sub + 1) * c), slice((n_sub - 1 - sub) * c, (n_sub - sub) * c))


def _scan(q, kf, kb, v, gf, gb, *, batch, seq, heads, dk, dv, chunk):
    n = batch * seq
    c = min(chunk, seq)
    rows = min(SCAN_CHUNKS_PER_STEP * c, seq)
    nb = seq // rows
    kw, vw = heads * dk, heads * dv
    fwd = lambda blk: (lambda b, i: (b * nb + i, blk))
    bwd = lambda blk: (lambda b, i: (b * nb + (nb - 1 - i), blk))
    out_f = lambda b, i: (b * nb + i, 0)
    out_b = lambda b, i: (b * nb + (nb - 1 - i), 0)
    return pl.pallas_call(
        functools.partial(_scan_kernel, heads=heads, dk=dk, dv=dv, c=c),
        grid=(batch, nb),
        in_specs=[
            pl.BlockSpec((rows, kw), fwd(q[1])), pl.BlockSpec((rows, kw), fwd(kf[1])),
            pl.BlockSpec((rows, vw), fwd(v[1])), pl.BlockSpec((rows, kw), fwd(gf[1])),
            pl.BlockSpec((rows, kw), bwd(q[1])), pl.BlockSpec((rows, kw), bwd(kb[1])),
            pl.BlockSpec((rows, vw), bwd(v[1])), pl.BlockSpec((rows, kw), bwd(gb[1])),
        ],
        out_specs=[pl.BlockSpec((rows, vw), out_f), pl.BlockSpec((rows, vw), out_b)],
        out_shape=[jax.ShapeDtypeStruct((n, vw), BF16), jax.ShapeDtypeStruct((n, vw), BF16)],
        scratch_shapes=[
            pltpu.VMEM((heads, dv, dk), F32),
            pltpu.VMEM((heads, dv, dk), F32),
            pltpu.VMEM((2, c, dk), F32),
        ],
        compiler_params=pltpu.CompilerParams(
            dimension_semantics=("arbitrary", "arbitrary"), vmem_limit_bytes=VMEM_LIMIT),
        name=f"scan_h{heads}",
    )(q[0], kf[0], v[0], gf[0], q[0], kb[0], v[0], gb[0])


def _outproj_kernel(ofg_ref, obg_ref, ofh_ref, obh_ref, gog_ref, hog_ref, h_ref, w_ref,
                    gng_ref, hng_ref, lng_ref, lnb_ref, o_ref, y_ref):
    def head_group(of_ref, ob_ref, gate_ref, g_ref, heads, dv, base):
        for hh in range(heads):
            sl = slice(hh * dv, (hh + 1) * dv)
            o = of_ref[:, sl].astype(F32) + ob_ref[:, sl].astype(F32)
            ms = jnp.mean(o * o, axis=-1, keepdims=True)
            y = o * lax.rsqrt(ms + RMS_EPS) * g_ref[...]
            gate = gate_ref[:, sl].astype(F32)
            y = y * (gate * (1.0 / (1.0 + jnp.exp(-gate))))
            y_ref[:, base + hh * dv: base + (hh + 1) * dv] = y.astype(BF16)

    head_group(ofg_ref, obg_ref, gog_ref, gng_ref, GLA_HEADS, GLA_DV, 0)
    head_group(ofh_ref, obh_ref, hog_ref, hng_ref, HG_HEADS, HG_DIM, GLA_WIDTH)
    m = jnp.dot(y_ref[...], w_ref[...], preferred_element_type=F32)
    o_ref[...] = _layernorm(ALPHA * h_ref[...] + m, lng_ref[...], lnb_ref[...])


def _outproj(ofg, obg, ofh, obh, z, h, layer, w_out, gng, hng, lng, lnb):
    n = h.shape[0]
    tm = min(OUT_TM, n)
    half = lambda i: (i, 0)
    const = lambda i: (0, 0)
    return pl.pallas_call(
        _outproj_kernel,
        grid=(n // tm,),
        in_specs=[
            pl.BlockSpec((tm, GLA_WIDTH), half), pl.BlockSpec((tm, GLA_WIDTH), half),
            pl.BlockSpec((tm, HG_WIDTH), half), pl.BlockSpec((tm, HG_WIDTH), half),
            pl.BlockSpec((tm, GLA_WIDTH), lambda i: (i, 2048 // GLA_WIDTH)),
            pl.BlockSpec((tm, HG_WIDTH), lambda i: (i, 7168 // HG_WIDTH)),
            pl.BlockSpec((tm, D_MODEL), half),
            pl.BlockSpec((None, D_MODEL, D_MODEL), lambda i: (layer, 0, 0), pipeline_mode=pl.Buffered(1)),
            pl.BlockSpec((1, GLA_DV), const), pl.BlockSpec((1, HG_DIM), const),
            pl.BlockSpec((1, D_MODEL), const), pl.BlockSpec((1, D_MODEL), const),
        ],
        out_specs=pl.BlockSpec((tm, D_MODEL), half),
        out_shape=jax.ShapeDtypeStruct((n, D_MODEL), F32),
        scratch_shapes=[pltpu.VMEM((tm, D_MODEL), BF16)],
        compiler_params=pltpu.CompilerParams(
            dimension_semantics=("arbitrary",), vmem_limit_bytes=VMEM_LIMIT),
        name="outproj_ln1",
    )(ofg, obg, ofh, obh, z, z, h, w_out, gng, hng, lng, lnb)


def _mlp_kernel(h_ref, wup_ref, wdn_ref, lng_ref, lnb_ref, o_ref, xb_ref, a_ref):
    j = pl.program_id(1)

    def ff_tile():
        for c0 in range(0, a_ref.shape[1], MXU_COLS):
            cols = slice(c0, c0 + MXU_COLS)
            a = jnp.dot(xb_ref[...], wup_ref[:, cols], preferred_element_type=F32)
            a_ref[:, cols] = jnp.square(jnp.maximum(a, 0.0)).astype(BF16)
        return jnp.dot(a_ref[...], wdn_ref[...], preferred_element_type=F32)

    @pl.when(j == 0)
    def _():
        xb_ref[...] = h_ref[...].astype(BF16)
        o_ref[...] = ff_tile()

    @pl.when(j > 0)
    def _():
        o_ref[...] += ff_tile()

    @pl.when(j == pl.num_programs(1) - 1)
    def _():
        for r0 in range(0, o_ref.shape[0], LN_ROWS):
            rows = slice(r0, r0 + LN_ROWS)
            o_ref[rows, :] = _layernorm(ALPHA * h_ref[rows, :] + o_ref[rows, :], lng_ref[...], lnb_ref[...])


def _mlp(h, layer, w_up, w_down, lng, lnb):
    n = h.shape[0]
    tm = min(MLP_TM, n)
    return pl.pallas_call(
        _mlp_kernel,
        grid=(n // tm, D_FF // MLP_TF),
        in_specs=[
            pl.BlockSpec((tm, D_MODEL), lambda i, j: (i, 0)),
            pl.BlockSpec((None, D_MODEL, MLP_TF), lambda i, j: (layer, 0, j)),
            pl.BlockSpec((None, MLP_TF, D_MODEL), lambda i, j: (layer, j, 0)),
            pl.BlockSpec((1, D_MODEL), lambda i, j: (0, 0)),
            pl.BlockSpec((1, D_MODEL), lambda i, j: (0, 0)),
        ],
        out_specs=pl.BlockSpec((tm, D_MODEL), lambda i, j: (i, 0)),
        out_shape=jax.ShapeDtypeStruct((n, D_MODEL), F32),
        scratch_shapes=[pltpu.VMEM((tm, D_MODEL), BF16), pltpu.VMEM((tm, MLP_TF), BF16)],
        compiler_params=pltpu.CompilerParams(
            dimension_semantics=("arbitrary", "arbitrary"), vmem_limit_bytes=MLP_VMEM_LIMIT),
        name="mlp_ln2",
    )(h, w_up, w_down, lng, lnb)


def _lower_bounds(p):
    c = jnp.cumsum(jax.nn.softmax(p.astype(F32), axis=0), axis=0)
    return c - c[0:1]


def _prep_params(w_in, gla_w_lr2, gla_b_lr, gla_norm_g, hg_norm_g, lbs_f, lbs_b,
                 w_out, ln1_g, ln1_b, w_up, w_down, ln2_g, ln2_b):
    glr0 = 2 * GLA_KWIDTH + 2 * GLA_WIDTH
    w_in = w_in.astype(BF16)
    w_main = jnp.concatenate([w_in[..., :glr0], w_in[..., glr0 + 2 * GLA_RANK:]], axis=-1)
    w_glr = jnp.pad(w_in[..., glr0:glr0 + 2 * GLA_RANK], ((0, 0), (0, 0), (0, GLR_PAD - 2 * GLA_RANK)))
    stacked = dict(w_main=w_main, w_glr=w_glr, w_out=w_out.astype(BF16),
                   w_up=w_up.astype(BF16), w_down=w_down.astype(BF16))
    layers = []
    for l in range(DEPTH):
        w2 = gla_w_lr2[l].astype(F32)
        w2c = jnp.zeros((GLR_PAD, 2 * GLA_KWIDTH), F32)
        w2c = w2c.at[:GLA_RANK, :GLA_KWIDTH].set(w2[0]).at[GLA_RANK:2 * GLA_RANK, GLA_KWIDTH:].set(w2[1])
        lb = jnp.concatenate([lbs_f[l], lbs_b[l]])[None, :]
        row = lambda a: a[l].astype(F32)[None, :]
        layers.append(dict(
            w2c=w2c.astype(BF16), b2c=gla_b_lr[l].astype(F32).reshape(1, 2 * GLA_KWIDTH),
            gate_c=jnp.concatenate([lb, 1.0 - lb, jnp.log(lb), jnp.log1p(-lb),
                                    jnp.zeros((4, 2 * HG_WIDTH), F32)], axis=0),
            gng=row(gla_norm_g), hng=row(hg_norm_g), ln1_g=row(ln1_g), ln1_b=row(ln1_b),
            ln2_g=row(ln2_g), ln2_b=row(ln2_b)))
    return stacked, layers


def _trunk(x, stacked, layers):
    batch, seq, _ = x.shape
    h = x.reshape(batch * seq, D_MODEL).astype(F32)
    for l, p in enumerate(layers):
        z, logf, ga = _inproj(h, l, stacked["w_main"], stacked["w_glr"], p["w2c"], p["b2c"], p["gate_c"])
        ofg, obg = _scan((z, 0), (z, 1), (z, 1), (z, 1), (ga, 0), (ga, 1),
                         batch=batch, seq=seq, heads=GLA_HEADS, dk=GLA_DK, dv=GLA_DV, chunk=GLA_CHUNK)
        ofh, obh = _scan((z, 3), (z, 4), (z, 5), (z, 6), (logf, 0), (logf, 1),
                         batch=batch, seq=seq, heads=HG_HEADS, dk=HG_DIM, dv=HG_DIM, chunk=HG_CHUNK)
        h = _outproj(ofg, obg, ofh, obh, z, h, l, stacked["w_out"], p["gng"], p["hng"], p["ln1_g"], p["ln1_b"])
        h = _mlp(h, l, stacked["w_up"], stacked["w_down"], p["ln2_g"], p["ln2_b"])
    return h.reshape(batch, seq, D_MODEL).astype(x.dtype)


def kernel(x_prompt, x_sample, w_in, gla_w_lr2, gla_b_lr, gla_norm_g, hg_norm_g, lower_bounds,
           w_out, ln1_g, ln1_b, w_up, w_down, ln2_g, ln2_b):
    lbs_f = _lower_bounds(lower_bounds[0])
    lbs_b = _lower_bounds(lower_bounds[1])
    stacked, layers = _prep_params(w_in, gla_w_lr2, gla_b_lr, gla_norm_g, hg_norm_g, lbs_f, lbs_b,
                                   w_out, ln1_g, ln1_b, w_up, w_down, ln2_g, ln2_b)
    return (_trunk(x_prompt, stacked, layers), _trunk(x_sample, stacked, layers))
```

```python
import functools

import jax
import jax.numpy as jnp
from jax import lax
from jax.experimental import pallas as pl
from jax.experimental.pallas import tpu as pltpu

F32 = jnp.float32
BF16 = jnp.bfloat16

D_MODEL = 2048
DEPTH = 2
GLA_HEADS = 4
GLA_WIDTH = D_MODEL // 2
GLA_DV = GLA_WIDTH // GLA_HEADS
GLA_DK = GLA_DV // 2
GLA_KWIDTH = GLA_HEADS * GLA_DK
GLA_RANK = 16
GLA_TAU = 16.0
HG_DIM = 128
HG_WIDTH = D_MODEL - GLA_WIDTH
HG_HEADS = HG_WIDTH // HG_DIM
D_FF = 4 * D_MODEL
LN_EPS = 1e-5
RMS_EPS = 1e-6
F32_TINY = 1.1754943508222875e-38
ALPHA = (2.0 * DEPTH) ** 0.25

LANES = 128
GLR_PAD = LANES
Z_WIDTH = 8192
VMEM_LIMIT = 48 * 1024 * 1024
IN_VMEM_LIMIT = 58 * 1024 * 1024

IN_TN = 1024
IN_TM = 1024
MXU_COLS = 256
GATE_TILE_LO = 4096 // IN_TN
GATE_TILE_HI = 6144 // IN_TN

GLA_CHUNK = 256
HG_CHUNK = 128
SCAN_CHUNKS_PER_STEP = 4
MAX_HALF_RANGE = 80.0

OUT_TM = 512
MLP_TM = 1024
MLP_TF = 1024
LN_ROWS = 128
MLP_VMEM_LIMIT = 60 * 1024 * 1024

_NT = (((1,), (1,)), ((), ()))
_TN = (((0,), (0,)), ((), ()))


def _log_sigmoid(x):
    return jnp.minimum(x, 0.0) - jnp.log(1.0 + jnp.exp(-jnp.abs(x)))


def _layernorm(r, g, b):
    mu = jnp.mean(r, axis=-1, keepdims=True)
    d = r - mu
    var = jnp.mean(d * d, axis=-1, keepdims=True)
    return d * lax.rsqrt(var + LN_EPS) * g + b


def _inproj_kernel(x_ref, w_ref, wglr_ref, w2_ref, b2_ref, gc_ref, z_ref, logf_ref, ga_ref, xb_ref):
    j = pl.program_id(1)
    is_gate = jnp.logical_and(j >= GATE_TILE_LO, j < GATE_TILE_HI)
    col_groups = lambda width: [slice(c, c + MXU_COLS) for c in range(0, width, MXU_COLS)]

    last = pl.num_programs(1) - 1

    def project_plain():
        for cols in col_groups(z_ref.shape[1]):
            z_ref[:, cols] = jnp.dot(xb_ref[...], w_ref[:, cols], preferred_element_type=F32).astype(BF16)

    @pl.when(j == 0)
    def _():
        xb = x_ref[...].astype(BF16)
        xb_ref[...] = xb
        for cols in col_groups(z_ref.shape[1]):
            zc = jnp.dot(xb, w_ref[:, cols], preferred_element_type=F32)
            if cols.stop <= GLA_KWIDTH:
                zc = zc * (GLA_DK ** -0.5)
            z_ref[:, cols] = zc.astype(BF16)

    @pl.when(jnp.logical_and(jnp.logical_and(j > 0, j < last), jnp.logical_not(is_gate)))
    def _():
        project_plain()

    @pl.when(j == last)
    def _():
        project_plain()
        glr = jnp.dot(xb_ref[...], wglr_ref[...], preferred_element_type=F32).astype(BF16)
        for cols in col_groups(ga_ref.shape[1]):
            a = jnp.dot(glr, w2_ref[:, cols], preferred_element_type=F32) + b2_ref[:, cols]
            ga_ref[:, cols] = _log_sigmoid(a) * (1.0 / GLA_TAU)

    @pl.when(is_gate)
    def _():
        f_min = None
        for cols in col_groups(z_ref.shape[1]):
            lb = gc_ref[0:1, cols]
            one_minus_lb = gc_ref[1:2, cols]
            acc = jnp.dot(xb_ref[...], w_ref[:, cols], preferred_element_type=F32)
            e = jnp.exp(-jnp.abs(acc))
            r = 1.0 / (1.0 + e)
            er = e * r
            pos = acc >= 0.0
            f = lb + one_minus_lb * jnp.where(pos, r, er)
            logf_ref[:, cols] = jnp.log(f)
            z_ref[:, cols] = (one_minus_lb * jnp.where(pos, er, r)).astype(BF16)
            f_min = f if f_min is None else jnp.minimum(f_min, f)

        @pl.when(jnp.logical_not(jnp.min(f_min) >= F32_TINY))
        def _():
            acc = jnp.dot(xb_ref[...], w_ref[...], preferred_element_type=F32)
            log_lb = gc_ref[2:3, :]
            x2 = gc_ref[3:4, :] + _log_sigmoid(acc)
            logf_ref[...] = (jnp.maximum(log_lb, x2)
                             + jnp.log(1.0 + jnp.exp(-jnp.abs(log_lb - x2))))


def _inproj(x, layer, w_main, w_glr, w2c, b2c, gate_c):
    n = x.shape[0]
    tm = min(IN_TM, n)
    n_gate = GATE_TILE_HI - GATE_TILE_LO
    gate_idx = lambda i, j: jnp.clip(j - GATE_TILE_LO, 0, n_gate - 1)
    return pl.pallas_call(
        _inproj_kernel,
        grid=(n // tm, Z_WIDTH // IN_TN),
        in_specs=[
            pl.BlockSpec((tm, D_MODEL), lambda i, j: (i, 0)),
            pl.BlockSpec((None, D_MODEL, IN_TN), lambda i, j: (layer, 0, j)),
            pl.BlockSpec((None, D_MODEL, GLR_PAD), lambda i, j: (layer, 0, 0)),
            pl.BlockSpec((GLR_PAD, 2 * GLA_KWIDTH), lambda i, j: (0, 0)),
            pl.BlockSpec((1, 2 * GLA_KWIDTH), lambda i, j: (0, 0)),
            pl.BlockSpec((8, IN_TN), lambda i, j: (0, gate_idx(i, j))),
        ],
        out_specs=[
            pl.BlockSpec((tm, IN_TN), lambda i, j: (i, j)),
            pl.BlockSpec((tm, IN_TN), lambda i, j: (i, gate_idx(i, j))),
            pl.BlockSpec((tm, 2 * GLA_KWIDTH), lambda i, j: (i, 0)),
        ],
        out_shape=[
            jax.ShapeDtypeStruct((n, Z_WIDTH), BF16),
            jax.ShapeDtypeStruct((n, 2 * HG_WIDTH), F32),
            jax.ShapeDtypeStruct((n, 2 * GLA_KWIDTH), F32),
        ],
        scratch_shapes=[pltpu.VMEM((tm, D_MODEL), BF16)],
        compiler_params=pltpu.CompilerParams(
            dimension_semantics=("arbitrary", "arbitrary"), vmem_limit_bytes=IN_VMEM_LIMIT),
        name="inproj",
    )(x, w_main, w_glr, w2c, b2c, gate_c)


def _cumsum_rows(tri, g):
    hi = g.astype(BF16)
    lo = (g - hi.astype(F32)).astype(BF16)
    return jnp.dot(jnp.concatenate([tri, tri], axis=1), jnp.concatenate([hi, lo], axis=0),
                   preferred_element_type=F32)


def _dir_fast(q, k, v, b, tot, st, mask):
    m = 0.5 * tot
    qt = q * jnp.exp(b - m).astype(BF16)
    kt = k * jnp.exp(m - b).astype(BF16)
    c = q.shape[0]
    ssc = (st * jnp.exp(m)).astype(BF16)
    r = lax.dot_general(qt, jnp.concatenate([kt, ssc], axis=0), _NT, preferred_element_type=F32)
    s = jnp.where(mask, r[:, :c], 0.0).astype(BF16)
    o = jnp.dot(s, v, preferred_element_type=F32) + r[:, c:]
    u = lax.dot_general(v, kt, _TN, preferred_element_type=F32)
    return o, st * jnp.exp(tot) + u * jnp.exp(tot - m)


def _dir_slow(q, k, v, b, tot, st, reverse, row_ref):
    c = q.shape[0]
    q = q.astype(F32)
    k = k.astype(F32)
    row_ref[0] = b
    row_ref[1] = q
    jj = lax.broadcasted_iota(jnp.int32, (c, 1), 0)
    lane = lax.broadcasted_iota(jnp.int32, (c, c), 1)

    def body(i, s_t):
        bi = row_ref[0, pl.ds(i, 1), :]
        qi = row_ref[1, pl.ds(i, 1), :]
        valid = (jj >= i) if reverse else (jj <= i)
        w = jnp.where(valid, jnp.exp(jnp.minimum(bi - b, 0.0)), 0.0)
        col = jnp.sum(qi * k * w, axis=-1, keepdims=True)
        return jnp.where(lane == i, col, s_t)

    s_t = lax.fori_loop(0, c, body, jnp.zeros((c, c), F32))
    o = lax.dot_general(s_t.astype(BF16), v, _TN, preferred_element_type=F32)
    qin = (q * jnp.exp(b)).astype(BF16)
    o = o + lax.dot_general(qin, st.astype(BF16), _NT, preferred_element_type=F32)
    kout = (k * jnp.exp(tot - b)).astype(BF16)
    u = lax.dot_general(v, kout, _TN, preferred_element_type=F32)
    return o, st * jnp.exp(tot) + u


def _scan_kernel(qf_ref, kf_ref, vf_ref, gf_ref, qb_ref, kb_ref, vb_ref, gb_ref,
                 of_ref, ob_ref, sf_ref, sb_ref, row_ref, *, heads, dk, dv, c):
    @pl.when(pl.program_id(1) == 0)
    def _():
        sf_ref[...] = jnp.zeros_like(sf_ref)
        sb_ref[...] = jnp.zeros_like(sb_ref)

    row = lax.broadcasted_iota(jnp.int32, (c, c), 0)
    col = lax.broadcasted_iota(jnp.int32, (c, c), 1)
    lower = row >= col
    upper = row <= col

    n_sub = qf_ref.shape[0] // c
    chunks = [(slice(sub * c, (sub + 1) * c), slice((n_sub - 1 - sub) * c, (n_sub - sub) * c))
              for sub in range(n_sub)]
    ltri = jnp.where(lower, 1.0, 0.0).astype(BF16)
    utri = jnp.where(upper, 1.0, 0.0).astype(BF16)
    cums = [(_cumsum_rows(ltri, gf_ref[rf, :]), _cumsum_rows(utri, gb_ref[rb, :])) for rf, rb in chunks]
    span = functools.reduce(jnp.maximum, [jnp.maximum(jnp.max(-bf[c - 1:c, :]), jnp.max(-bb[0:1, :]))
                                          for bf, bb in cums])
    fast = span < 2.0 * MAX_HALF_RANGE

    def run(step_f, step_b):
        for (rf, rb), (bf_all, bb_all) in zip(chunks, cums):
            tot_f = bf_all[c - 1:c, :]
            tot_b = bb_all[0:1, :]
            for h in range(heads):
                ks = slice(h * dk, (h + 1) * dk)
                vs = slice(h * dv, (h + 1) * dv)
                o, s_new = step_f(qf_ref[rf, ks], kf_ref[rf, ks], vf_ref[rf, vs],
                                  bf_all[:, ks], tot_f[:, ks], sf_ref[h])
                of_ref[rf, vs] = o.astype(BF16)
                sf_ref[h] = s_new
                o, s_new = step_b(qb_ref[rb, ks], kb_ref[rb, ks], vb_ref[rb, vs],
                                  bb_all[:, ks], tot_b[:, ks], sb_ref[h])
                ob_ref[rb, vs] = o.astype(BF16)
                sb_ref[h] = s_new

    @pl.when(fast)
    def _():
        run(functools.partial(_dir_fast, mask=lower), functools.partial(_dir_fast, mask=upper))

    @pl.when(jnp.logical_not(fast))
    def _():
        run(functools.partial(_dir_slow, reverse=False, row_ref=row_ref),
            functools.partial(_dir_slow, reverse=True, row_ref=row_ref))


def _scan(q, kf, kb, v, gf, gb, *, batch, seq, heads, dk, dv, chunk):
    n = batch * seq
    c = min(chunk, seq)
    rows = min(SCAN_CHUNKS_PER_STEP * c, seq)
    nb = seq // rows
    kw, vw = heads * dk, heads * dv
    fwd = lambda blk: (lambda b, i: (b * nb + i, blk))
    bwd = lambda blk: (lambda b, i: (b * nb + (nb - 1 - i), blk))
    out_f = lambda b, i: (b * nb + i, 0)
    out_b = lambda b, i: (b * nb + (nb - 1 - i), 0)
    return pl.pallas_call(
        functools.partial(_scan_kernel, heads=heads, dk=dk, dv=dv, c=c),
        grid=(batch, nb),
        in_specs=[
            pl.BlockSpec((rows, kw), fwd(q[1])), pl.BlockSpec((rows, kw), fwd(kf[1])),
            pl.BlockSpec((rows, vw), fwd(v[1])), pl.BlockSpec((rows, kw), fwd(gf[1])),
            pl.BlockSpec((rows, kw), bwd(q[1])), pl.BlockSpec((rows, kw), bwd(kb[1])),
            pl.BlockSpec((rows, vw), bwd(v[1])), pl.BlockSpec((rows, kw), bwd(gb[1])),
        ],
        out_specs=[pl.BlockSpec((rows, vw), out_f), pl.BlockSpec((rows, vw), out_b)],
        out_shape=[jax.ShapeDtypeStruct((n, vw), BF16), jax.ShapeDtypeStruct((n, vw), BF16)],
        scratch_shapes=[
            pltpu.VMEM((heads, dv, dk), F32),
            pltpu.VMEM((heads, dv, dk), F32),
            pltpu.VMEM((2, c, dk), F32),
        ],
        compiler_params=pltpu.CompilerParams(
            dimension_semantics=("arbitrary", "arbitrary"), vmem_limit_bytes=VMEM_LIMIT),
        name=f"scan_h{heads}",
    )(q[0], kf[0], v[0], gf[0], q[0], kb[0], v[0], gb[0])


def _outproj_kernel(ofg_ref, obg_ref, ofh_ref, obh_ref, gog_ref, hog_ref, h_ref, w_ref,
                    gng_ref, hng_ref, lng_ref, lnb_ref, o_ref, y_ref):
    def head_group(of_ref, ob_ref, gate_ref, g_ref, heads, dv, base):
        for hh in range(heads):
            sl = slice(hh * dv, (hh + 1) * dv)
            o = of_ref[:, sl].astype(F32) + ob_ref[:, sl].astype(F32)
            ms = jnp.mean(o * o, axis=-1, keepdims=True)
            y = o * lax.rsqrt(ms + RMS_EPS) * g_ref[...]
            gate = gate_ref[:, sl].astype(F32)
            y = y * (gate * (1.0 / (1.0 + jnp.exp(-gate))))
            y_ref[:, base + hh * dv: base + (hh + 1) * dv] = y.astype(BF16)

    head_group(ofg_ref, obg_ref, gog_ref, gng_ref, GLA_HEADS, GLA_DV, 0)
    head_group(ofh_ref, obh_ref, hog_ref, hng_ref, HG_HEADS, HG_DIM, GLA_WIDTH)
    m = jnp.dot(y_ref[...], w_ref[...], preferred_element_type=F32)
    o_ref[...] = _layernorm(ALPHA * h_ref[...] + m, lng_ref[...], lnb_ref[...])


def _outproj(ofg, obg, ofh, obh, z, h, layer, w_out, gng, hng, lng, lnb):
    n = h.shape[0]
    tm = min(OUT_TM, n)
    half = lambda i: (i, 0)
    const = lambda i: (0, 0)
    return pl.pallas_call(
        _outproj_kernel,
        grid=(n // tm,),
        in_specs=[
            pl.BlockSpec((tm, GLA_WIDTH), half), pl.BlockSpec((tm, GLA_WIDTH), half),
            pl.BlockSpec((tm, HG_WIDTH), half), pl.BlockSpec((tm, HG_WIDTH), half),
            pl.BlockSpec((tm, GLA_WIDTH), lambda i: (i, 2048 // GLA_WIDTH)),
            pl.BlockSpec((tm, HG_WIDTH), lambda i: (i, 7168 // HG_WIDTH)),
            pl.BlockSpec((tm, D_MODEL), half),
            pl.BlockSpec((None, D_MODEL, D_MODEL), lambda i: (layer, 0, 0), pipeline_mode=pl.Buffered(1)),
            pl.BlockSpec((1, GLA_DV), const), pl.BlockSpec((1, HG_DIM), const),
            pl.BlockSpec((1, D_MODEL), const), pl.BlockSpec((1, D_MODEL), const),
        ],
        out_specs=pl.BlockSpec((tm, D_MODEL), half),
        out_shape=jax.ShapeDtypeStruct((n, D_MODEL), F32),
        scratch_shapes=[pltpu.VMEM((tm, D_MODEL), BF16)],
        compiler_params=pltpu.CompilerParams(
            dimension_semantics=("arbitrary",), vmem_limit_bytes=VMEM_LIMIT),
        name="outproj_ln1",
    )(ofg, obg, ofh, obh, z, z, h, w_out, gng, hng, lng, lnb)


def _mlp_kernel(h_ref, wup_ref, wdn_ref, lng_ref, lnb_ref, o_ref, xb_ref, a_ref):
    j = pl.program_id(1)

    def ff_tile():
        for c0 in range(0, a_ref.shape[1], MXU_COLS):
            cols = slice(c0, c0 + MXU_COLS)
            a = jnp.dot(xb_ref[...], wup_ref[:, cols], preferred_element_type=F32)
            a_ref[:, cols] = jnp.square(jnp.maximum(a, 0.0)).astype(BF16)
        return jnp.dot(a_ref[...], wdn_ref[...], preferred_element_type=F32)

    @pl.when(j == 0)
    def _():
        xb_ref[...] = h_ref[...].astype(BF16)
        o_ref[...] = ff_tile()

    @pl.when(j > 0)
    def _():
        o_ref[...] += ff_tile()

    @pl.when(j == pl.num_programs(1) - 1)
    def _():
        for r0 in range(0, o_ref.shape[0], LN_ROWS):
            rows = slice(r0, r0 + LN_ROWS)
            o_ref[rows, :] = _layernorm(ALPHA * h_ref[rows, :] + o_ref[rows, :], lng_ref[...], lnb_ref[...])


def _mlp(h, layer, w_up, w_down, lng, lnb):
    n = h.shape[0]
    tm = min(MLP_TM, n)
    return pl.pallas_call(
        _mlp_kernel,
        grid=(n // tm, D_FF // MLP_TF),
        in_specs=[
            pl.BlockSpec((tm, D_MODEL), lambda i, j: (i, 0)),
            pl.BlockSpec((None, D_MODEL, MLP_TF), lambda i, j: (layer, 0, j)),
            pl.BlockSpec((None, MLP_TF, D_MODEL), lambda i, j: (layer, j, 0)),
            pl.BlockSpec((1, D_MODEL), lambda i, j: (0, 0)),
            pl.BlockSpec((1, D_MODEL), lambda i, j: (0, 0)),
        ],
        out_specs=pl.BlockSpec((tm, D_MODEL), lambda i, j: (i, 0)),
        out_shape=jax.ShapeDtypeStruct((n, D_MODEL), F32),
        scratch_shapes=[pltpu.VMEM((tm, D_MODEL), BF16), pltpu.VMEM((tm, MLP_TF), BF16)],
        compiler_params=pltpu.CompilerParams(
            dimension_semantics=("arbitrary", "arbitrary"), vmem_limit_bytes=MLP_VMEM_LIMIT),
        name="mlp_ln2",
    )(h, w_up, w_down, lng, lnb)


def _lower_bounds(p):
    c = jnp.cumsum(jax.nn.softmax(p.astype(F32), axis=0), axis=0)
    return c - c[0:1]


def _prep_params(w_in, gla_w_lr2, gla_b_lr, gla_norm_g, hg_norm_g, lbs_f, lbs_b,
                 w_out, ln1_g, ln1_b, w_up, w_down, ln2_g, ln2_b):
    glr0 = 2 * GLA_KWIDTH + 2 * GLA_WIDTH
    w_in = w_in.astype(BF16)
    w_main = jnp.concatenate([w_in[..., :glr0], w_in[..., glr0 + 2 * GLA_RANK:]], axis=-1)
    w_glr = jnp.pad(w_in[..., glr0:glr0 + 2 * GLA_RANK], ((0, 0), (0, 0), (0, GLR_PAD - 2 * GLA_RANK)))
    stacked = dict(w_main=w_main, w_glr=w_glr, w_out=w_out.astype(BF16),
                   w_up=w_up.astype(BF16), w_down=w_down.astype(BF16))
    layers = []
    for l in range(DEPTH):
        w2 = gla_w_lr2[l].astype(F32)
        w2c = jnp.zeros((GLR_PAD, 2 * GLA_KWIDTH), F32)
        w2c = w2c.at[:GLA_RANK, :GLA_KWIDTH].set(w2[0]).at[GLA_RANK:2 * GLA_RANK, GLA_KWIDTH:].set(w2[1])
        lb = jnp.concatenate([lbs_f[l], lbs_b[l]])[None, :]
        row = lambda a: a[l].astype(F32)[None, :]
        layers.append(dict(
            w2c=w2c.astype(BF16), b2c=gla_b_lr[l].astype(F32).reshape(1, 2 * GLA_KWIDTH),
            gate_c=jnp.concatenate([lb, 1.0 - lb, jnp.log(lb), jnp.log1p(-lb),
                                    jnp.zeros((4, 2 * HG_WIDTH), F32)], axis=0),
            gng=row(gla_norm_g), hng=row(hg_norm_g), ln1_g=row(ln1_g), ln1_b=row(ln1_b),
            ln2_g=row(ln2_g), ln2_b=row(ln2_b)))
    return stacked, layers


def _trunk(x, stacked, layers):
    batch, seq, _ = x.shape
    h = x.reshape(batch * seq, D_MODEL).astype(F32)
    for l, p in enumerate(layers):
        z, logf, ga = _inproj(h, l, stacked["w_main"], stacked["w_glr"], p["w2c"], p["b2c"], p["gate_c"])
        ofg, obg = _scan((z, 0), (z, 1), (z, 1), (z, 1), (ga, 0), (ga, 1),
                         batch=batch, seq=seq, heads=GLA_HEADS, dk=GLA_DK, dv=GLA_DV, chunk=GLA_CHUNK)
        ofh, obh = _scan((z, 3), (z, 4), (z, 5), (z, 6), (logf, 0), (logf, 1),
                         batch=batch, seq=seq, heads=HG_HEADS, dk=HG_DIM, dv=HG_DIM, chunk=HG_CHUNK)
        h = _outproj(ofg, obg, ofh, obh, z, h, l, stacked["w_out"], p["gng"], p["hng"], p["ln1_g"], p["ln1_b"])
        h = _mlp(h, l, stacked["w_up"], stacked["w_down"], p["ln2_g"], p["ln2_b"])
    return h.reshape(batch, seq, D_MODEL).astype(x.dtype)


def kernel(x_prompt, x_sample, w_in, gla_w_lr2, gla_b_lr, gla_norm_g, hg_norm_g, lower_bounds,
           w_out, ln1_g, ln1_b, w_up, w_down, ln2_g, ln2_b):
    lbs_f = _lower_bounds(lower_bounds[0])
    lbs_b = _lower_bounds(lower_bounds[1])
    stacked, layers = _prep_params(w_in, gla_w_lr2, gla_b_lr, gla_norm_g, hg_norm_g, lbs_f, lbs_b,
                                   w_out, ln1_g, ln1_b, w_up, w_down, ln2_g, ln2_b)
    return (_trunk(x_prompt, stacked, layers), _trunk(x_sample, stacked, layers))
```

```python
import functools

import jax
import jax.numpy as jnp
from jax import lax
from jax.experimental import pallas as pl
from jax.experimental.pallas import tpu as pltpu

F32 = jnp.float32
BF16 = jnp.bfloat16

D_MODEL = 2048
DEPTH = 2
GLA_HEADS = 4
GLA_WIDTH = D_MODEL // 2
GLA_DV = GLA_WIDTH // GLA_HEADS
GLA_DK = GLA_DV // 2
GLA_KWIDTH = GLA_HEADS * GLA_DK
GLA_RANK = 16
GLA_TAU = 16.0
HG_DIM = 128
HG_WIDTH = D_MODEL - GLA_WIDTH
HG_HEADS = HG_WIDTH // HG_DIM
D_FF = 4 * D_MODEL
LN_EPS = 1e-5
RMS_EPS = 1e-6
F32_TINY = 1.1754943508222875e-38
ALPHA = (2.0 * DEPTH) ** 0.25

LANES = 128
GLR_PAD = LANES
Z_WIDTH = 8192
VMEM_LIMIT = 48 * 1024 * 1024
IN_VMEM_LIMIT = 58 * 1024 * 1024

IN_TN = 1024
IN_TM = 1024
MXU_COLS = 256
GATE_TILE_LO = 4096 // IN_TN
GATE_TILE_HI = 6144 // IN_TN

GLA_CHUNK = 256
HG_CHUNK = 128
SCAN_ROWS = 512
SCAN_VMEM_LIMIT = 58 * 1024 * 1024
MAX_HALF_RANGE = 80.0

OUT_TM = 512
MLP_TM = 1024
MLP_TF = 1024
LN_ROWS = 128
MLP_VMEM_LIMIT = 60 * 1024 * 1024

_NT = (((1,), (1,)), ((), ()))
_TN = (((0,), (0,)), ((), ()))


def _log_sigmoid(x):
    return jnp.minimum(x, 0.0) - jnp.log(1.0 + jnp.exp(-jnp.abs(x)))


def _layernorm(r, g, b):
    mu = jnp.mean(r, axis=-1, keepdims=True)
    d = r - mu
    var = jnp.mean(d * d, axis=-1, keepdims=True)
    return d * lax.rsqrt(var + LN_EPS) * g + b


def _inproj_kernel(x_ref, w_ref, wglr_ref, w2_ref, b2_ref, gc_ref, z_ref, logf_ref, ga_ref, xb_ref):
    j = pl.program_id(1)
    is_gate = jnp.logical_and(j >= GATE_TILE_LO, j < GATE_TILE_HI)
    col_groups = lambda width: [slice(c, c + MXU_COLS) for c in range(0, width, MXU_COLS)]

    last = pl.num_programs(1) - 1

    def project_plain():
        for cols in col_groups(z_ref.shape[1]):
            z_ref[:, cols] = jnp.dot(xb_ref[...], w_ref[:, cols], preferred_element_type=F32).astype(BF16)

    @pl.when(j == 0)
    def _():
        xb = x_ref[...].astype(BF16)
        xb_ref[...] = xb
        for cols in col_groups(z_ref.shape[1]):
            zc = jnp.dot(xb, w_ref[:, cols], preferred_element_type=F32)
            if cols.stop <= GLA_KWIDTH:
                zc = zc * (GLA_DK ** -0.5)
            z_ref[:, cols] = zc.astype(BF16)

    @pl.when(jnp.logical_and(jnp.logical_and(j > 0, j < last), jnp.logical_not(is_gate)))
    def _():
        project_plain()

    @pl.when(j == last)
    def _():
        project_plain()
        glr = jnp.dot(xb_ref[...], wglr_ref[...], preferred_element_type=F32).astype(BF16)
        for cols in col_groups(ga_ref.shape[1]):
            a = jnp.dot(glr, w2_ref[:, cols], preferred_element_type=F32) + b2_ref[:, cols]
            ga_ref[:, cols] = _log_sigmoid(a) * (1.0 / GLA_TAU)

    @pl.when(is_gate)
    def _():
        f_min = None
        for cols in col_groups(z_ref.shape[1]):
            lb = gc_ref[0:1, cols]
            one_minus_lb = gc_ref[1:2, cols]
            acc = jnp.dot(xb_ref[...], w_ref[:, cols], preferred_element_type=F32)
            e = jnp.exp(-jnp.abs(acc))
            r = 1.0 / (1.0 + e)
            er = e * r
            pos = acc >= 0.0
            f = lb + one_minus_lb * jnp.where(pos, r, er)
            logf_ref[:, cols] = jnp.log(f)
            z_ref[:, cols] = (one_minus_lb * jnp.where(pos, er, r)).astype(BF16)
            f_min = f if f_min is None else jnp.minimum(f_min, f)

        @pl.when(jnp.logical_not(jnp.min(f_min) >= F32_TINY))
        def _():
            acc = jnp.dot(xb_ref[...], w_ref[...], preferred_element_type=F32)
            log_lb = gc_ref[2:3, :]
            x2 = gc_ref[3:4, :] + _log_sigmoid(acc)
            logf_ref[...] = (jnp.maximum(log_lb, x2)
                             + jnp.log(1.0 + jnp.exp(-jnp.abs(log_lb - x2))))


def _inproj(x, layer, w_main, w_glr, w2c, b2c, gate_c):
    n = x.shape[0]
    tm = min(IN_TM, n)
    n_gate = GATE_TILE_HI - GATE_TILE_LO
    gate_idx = lambda i, j: jnp.clip(j - GATE_TILE_LO, 0, n_gate - 1)
    return pl.pallas_call(
        _inproj_kernel,
        grid=(n // tm, Z_WIDTH // IN_TN),
        in_specs=[
            pl.BlockSpec((tm, D_MODEL), lambda i, j: (i, 0)),
            pl.BlockSpec((None, D_MODEL, IN_TN), lambda i, j: (layer, 0, j)),
            pl.BlockSpec((None, D_MODEL, GLR_PAD), lambda i, j: (layer, 0, 0)),
            pl.BlockSpec((GLR_PAD, 2 * GLA_KWIDTH), lambda i, j: (0, 0)),
            pl.BlockSpec((1, 2 * GLA_KWIDTH), lambda i, j: (0, 0)),
            pl.BlockSpec((8, IN_TN), lambda i, j: (0, gate_idx(i, j))),
        ],
        out_specs=[
            pl.BlockSpec((tm, IN_TN), lambda i, j: (i, j)),
            pl.BlockSpec((tm, IN_TN), lambda i, j: (i, gate_idx(i, j))),
            pl.BlockSpec((tm, 2 * GLA_KWIDTH), lambda i, j: (i, 0)),
        ],
        out_shape=[
            jax.ShapeDtypeStruct((n, Z_WIDTH), BF16),
            jax.ShapeDtypeStruct((n, 2 * HG_WIDTH), F32),
            jax.ShapeDtypeStruct((n, 2 * GLA_KWIDTH), F32),
        ],
        scratch_shapes=[pltpu.VMEM((tm, D_MODEL), BF16)],
        compiler_params=pltpu.CompilerParams(
            dimension_semantics=("arbitrary", "arbitrary"), vmem_limit_bytes=IN_VMEM_LIMIT),
        name="inproj",
    )(x, w_main, w_glr, w2c, b2c, gate_c)


def _cumsum_rows(tri, g):
    hi = g.astype(BF16)
    lo = (g - hi.astype(F32)).astype(BF16)
    return jnp.dot(jnp.concatenate([tri, tri], axis=1), jnp.concatenate([hi, lo], axis=0),
                   preferred_element_type=F32)


def _dir_fast(q, k, v, b, tot, st, mask):
    m = 0.5 * tot
    qt = q * jnp.exp(b - m).astype(BF16)
    kt = k * jnp.exp(m - b).astype(BF16)
    c = q.shape[0]
    ssc = (st * jnp.exp(m)).astype(BF16)
    r = lax.dot_general(qt, jnp.concatenate([kt, ssc], axis=0), _NT, preferred_element_type=F32)
    s = jnp.where(mask, r[:, :c], 0.0).astype(BF16)
    o = jnp.dot(s, v, preferred_element_type=F32) + r[:, c:]
    u = lax.dot_general(v, kt, _TN, preferred_element_type=F32)
    return o, st * jnp.exp(tot) + u * jnp.exp(tot - m)


def _dir_slow(q, k, v, b, tot, st, reverse, row_ref):
    c = q.shape[0]
    q = q.astype(F32)
    k = k.astype(F32)
    row_ref[0] = b
    row_ref[1] = q
    jj = lax.broadcasted_iota(jnp.int32, (c, 1), 0)
    lane = lax.broadcasted_iota(jnp.int32, (c, c), 1)

    def body(i, s_t):
        bi = row_ref[0, pl.ds(i, 1), :]
        qi = row_ref[1, pl.ds(i, 1), :]
        valid = (jj >= i) if reverse else (jj <= i)
        w = jnp.where(valid, jnp.exp(jnp.minimum(bi - b, 0.0)), 0.0)
        col = jnp.sum(qi * k * w, axis=-1, keepdims=True)
        return jnp.where(lane == i, col, s_t)

    s_t = lax.fori_loop(0, c, body, jnp.zeros((c, c), F32))
    o = lax.dot_general(s_t.astype(BF16), v, _TN, preferred_element_type=F32)
    qin = (q * jnp.exp(b)).astype(BF16)
    o = o + lax.dot_general(qin, st.astype(BF16), _NT, preferred_element_type=F32)
    kout = (k * jnp.exp(tot - b)).astype(BF16)
    u = lax.dot_general(v, kout, _TN, preferred_element_type=F32)
    return o, st * jnp.exp(tot) + u


def _scan_kernel(*refs, groups):
    ng = len(groups)
    ins = [refs[8 * g: 8 * g + 8] for g in range(ng)]
    outs = [refs[8 * ng + 2 * g: 8 * ng + 2 * g + 2] for g in range(ng)]
    scr = [refs[10 * ng + 3 * g: 10 * ng + 3 * g + 3] for g in range(ng)]

    @pl.when(pl.program_id(1) == 0)
    def _():
        for sf_ref, sb_ref, _ in scr:
            sf_ref[...] = jnp.zeros_like(sf_ref)
            sb_ref[...] = jnp.zeros_like(sb_ref)

    plans = [_scan_group(*ins[g], *outs[g], *scr[g], heads=groups[g][0], dk=groups[g][1],
                         dv=groups[g][2], c=groups[g][3]) for g in range(ng)]
    fast = functools.reduce(jnp.maximum, [p[0] for p in plans]) < 2.0 * MAX_HALF_RANGE

    @pl.when(fast)
    def _():
        for _, run in plans:
            run(True)

    @pl.when(jnp.logical_not(fast))
    def _():
        for _, run in plans:
            run(False)


def _scan_group(qf_ref, kf_ref, vf_ref, gf_ref, qb_ref, kb_ref, vb_ref, gb_ref,
                of_ref, ob_ref, sf_ref, sb_ref, row_ref, *, heads, dk, dv, c):
    row = lax.broadcasted_iota(jnp.int32, (c, c), 0)
    col = lax.broadcasted_iota(jnp.int32, (c, c), 1)
    lower = row >= col
    upper = row <= col

    n_sub = qf_ref.shape[0] // c
    chunks = [(slice(sub * c, (sub + 1) * c), slice((n_sub - 1 - sub) * c, (n_sub - sub) * c))
              for sub in range(n_sub)]
    ltri = jnp.where(lower, 1.0, 0.0).astype(BF16)
    utri = jnp.where(upper, 1.0, 0.0).astype(BF16)
    cums = [(_cumsum_rows(ltri, gf_ref[rf, :]), _cumsum_rows(utri, gb_ref[rb, :])) for rf, rb in chunks]
    span = functools.reduce(jnp.maximum, [jnp.maximum(jnp.max(-bf[c - 1:c, :]), jnp.max(-bb[0:1, :]))
                                          for bf, bb in cums])

    def run(fast):
        if fast:
            step_f = functools.partial(_dir_fast, mask=lower)
            step_b = functools.partial(_dir_fast, mask=upper)
        else:
            step_f = functools.partial(_dir_slow, reverse=False, row_ref=row_ref)
            step_b = functools.partial(_dir_slow, reverse=True, row_ref=row_ref)
        for (rf, rb), (bf_all, bb_all) in zip(chunks, cums):
            tot_f = bf_all[c - 1:c, :]
            tot_b = bb_all[0:1, :]
            for h in range(heads):
                ks = slice(h * dk, (h + 1) * dk)
                vs = slice(h * dv, (h + 1) * dv)
                o, s_new = step_f(qf_ref[rf, ks], kf_ref[rf, ks], vf_ref[rf, vs],
                                  bf_all[:, ks], tot_f[:, ks], sf_ref[h])
                of_ref[rf, vs] = o.astype(BF16)
                sf_ref[h] = s_new
                o, s_new = step_b(qb_ref[rb, ks], kb_ref[rb, ks], vb_ref[rb, vs],
                                  bb_all[:, ks], tot_b[:, ks], sb_ref[h])
                ob_ref[rb, vs] = o.astype(BF16)
                sb_ref[h] = s_new

    return span, run


def _scan(group_operands, groups, *, batch, seq):
    n = batch * seq
    rows = min(SCAN_ROWS, seq)
    nb = seq // rows
    fwd = lambda blk: (lambda b, i: (b * nb + i, blk))
    bwd = lambda blk: (lambda b, i: (b * nb + (nb - 1 - i), blk))
    out_f = lambda b, i: (b * nb + i, 0)
    out_b = lambda b, i: (b * nb + (nb - 1 - i), 0)
    in_specs, args, out_specs, out_shape, scratch, static = [], [], [], [], [], []
    for (q, kf, kb, v, gf, gb), (heads, dk, dv, chunk) in zip(group_operands, groups):
        kw, vw, c = heads * dk, heads * dv, min(chunk, seq)
        in_specs += [pl.BlockSpec((rows, kw), fwd(q[1])), pl.BlockSpec((rows, kw), fwd(kf[1])),
                     pl.BlockSpec((rows, vw), fwd(v[1])), pl.BlockSpec((rows, kw), fwd(gf[1])),
                     pl.BlockSpec((rows, kw), bwd(q[1])), pl.BlockSpec((rows, kw), bwd(kb[1])),
                     pl.BlockSpec((rows, vw), bwd(v[1])), pl.BlockSpec((rows, kw), bwd(gb[1]))]
        args += [q[0], kf[0], v[0], gf[0], q[0], kb[0], v[0], gb[0]]
        out_specs += [pl.BlockSpec((rows, vw), out_f), pl.BlockSpec((rows, vw), out_b)]
        out_shape += [jax.ShapeDtypeStruct((n, vw), BF16), jax.ShapeDtypeStruct((n, vw), BF16)]
        scratch += [pltpu.VMEM((heads, dv, dk), F32), pltpu.VMEM((heads, dv, dk), F32),
                    pltpu.VMEM((2, c, dk), F32)]
        static.append((heads, dk, dv, c))
    outs = pl.pallas_call(
        functools.partial(_scan_kernel, groups=tuple(static)),
        grid=(batch, nb),
        in_specs=in_specs, out_specs=out_specs, out_shape=out_shape, scratch_shapes=scratch,
        compiler_params=pltpu.CompilerParams(
            dimension_semantics=("arbitrary", "arbitrary"), vmem_limit_bytes=SCAN_VMEM_LIMIT),
        name="scan",
    )(*args)
    return [tuple(outs[2 * g: 2 * g + 2]) for g in range(len(groups))]


def _outproj_kernel(ofg_ref, obg_ref, ofh_ref, obh_ref, gog_ref, hog_ref, h_ref, w_ref,
                    gng_ref, hng_ref, lng_ref, lnb_ref, o_ref, y_ref):
    def head_group(of_ref, ob_ref, gate_ref, g_ref, heads, dv, base):
        for hh in range(heads):
            sl = slice(hh * dv, (hh + 1) * dv)
            o = of_ref[:, sl].astype(F32) + ob_ref[:, sl].astype(F32)
            ms = jnp.mean(o * o, axis=-1, keepdims=True)
            y = o * lax.rsqrt(ms + RMS_EPS) * g_ref[...]
            gate = gate_ref[:, sl].astype(F32)
            y = y * (gate * (1.0 / (1.0 + jnp.exp(-gate))))
            y_ref[:, base + hh * dv: base + (hh + 1) * dv] = y.astype(BF16)

    head_group(ofg_ref, obg_ref, gog_ref, gng_ref, GLA_HEADS, GLA_DV, 0)
    head_group(ofh_ref, obh_ref, hog_ref, hng_ref, HG_HEADS, HG_DIM, GLA_WIDTH)
    m = jnp.dot(y_ref[...], w_ref[...], preferred_element_type=F32)
    o_ref[...] = _layernorm(ALPHA * h_ref[...] + m, lng_ref[...], lnb_ref[...])


def _outproj(ofg, obg, ofh, obh, z, h, layer, w_out, gng, hng, lng, lnb):
    n = h.shape[0]
    tm = min(OUT_TM, n)
    half = lambda i: (i, 0)
    const = lambda i: (0, 0)
    return pl.pallas_call(
        _outproj_kernel,
        grid=(n // tm,),
        in_specs=[
            pl.BlockSpec((tm, GLA_WIDTH), half), pl.BlockSpec((tm, GLA_WIDTH), half),
            pl.BlockSpec((tm, HG_WIDTH), half), pl.BlockSpec((tm, HG_WIDTH), half),
            pl.BlockSpec((tm, GLA_WIDTH), lambda i: (i, 2048 // GLA_WIDTH)),
            pl.BlockSpec((tm, HG_WIDTH), lambda i: (i, 7168 // HG_WIDTH)),
            pl.BlockSpec((tm, D_MODEL), half),
            pl.BlockSpec((None, D_MODEL, D_MODEL), lambda i: (layer, 0, 0), pipeline_mode=pl.Buffered(1)),
            pl.BlockSpec((1, GLA_DV), const), pl.BlockSpec((1, HG_DIM), const),
            pl.BlockSpec((1, D_MODEL), const), pl.BlockSpec((1, D_MODEL), const),
        ],
        out_specs=pl.BlockSpec((tm, D_MODEL), half),
        out_shape=jax.ShapeDtypeStruct((n, D_MODEL), F32),
        scratch_shapes=[pltpu.VMEM((tm, D_MODEL), BF16)],
        compiler_params=pltpu.CompilerParams(
            dimension_semantics=("arbitrary",), vmem_limit_bytes=VMEM_LIMIT),
        name="outproj_ln1",
    )(ofg, obg, ofh, obh, z, z, h, w_out, gng, hng, lng, lnb)


def _mlp_kernel(h_ref, wup_ref, wdn_ref, lng_ref, lnb_ref, o_ref, xb_ref, a_ref):
    j = pl.program_id(1)

    def ff_tile():
        for c0 in range(0, a_ref.shape[1], MXU_COLS):
            cols = slice(c0, c0 + MXU_COLS)
            a = jnp.dot(xb_ref[...], wup_ref[:, cols], preferred_element_type=F32)
            a_ref[:, cols] = jnp.square(jnp.maximum(a, 0.0)).astype(BF16)
        return jnp.dot(a_ref[...], wdn_ref[...], preferred_element_type=F32)

    @pl.when(j == 0)
    def _():
        xb_ref[...] = h_ref[...].astype(BF16)
        o_ref[...] = ff_tile()

    @pl.when(j > 0)
    def _():
        o_ref[...] += ff_tile()

    @pl.when(j == pl.num_programs(1) - 1)
    def _():
        for r0 in range(0, o_ref.shape[0], LN_ROWS):
            rows = slice(r0, r0 + LN_ROWS)
            o_ref[rows, :] = _layernorm(ALPHA * h_ref[rows, :] + o_ref[rows, :], lng_ref[...], lnb_ref[...])


def _mlp(h, layer, w_up, w_down, lng, lnb):
    n = h.shape[0]
    tm = min(MLP_TM, n)
    return pl.pallas_call(
        _mlp_kernel,
        grid=(n // tm, D_FF // MLP_TF),
        in_specs=[
            pl.BlockSpec((tm, D_MODEL), lambda i, j: (i, 0)),
            pl.BlockSpec((None, D_MODEL, MLP_TF), lambda i, j: (layer, 0, j)),
            pl.BlockSpec((None, MLP_TF, D_MODEL), lambda i, j: (layer, j, 0)),
            pl.BlockSpec((1, D_MODEL), lambda i, j: (0, 0)),
            pl.BlockSpec((1, D_MODEL), lambda i, j: (0, 0)),
        ],
        out_specs=pl.BlockSpec((tm, D_MODEL), lambda i, j: (i, 0)),
        out_shape=jax.ShapeDtypeStruct((n, D_MODEL), F32),
        scratch_shapes=[pltpu.VMEM((tm, D_MODEL), BF16), pltpu.VMEM((tm, MLP_TF), BF16)],
        compiler_params=pltpu.CompilerParams(
            dimension_semantics=("arbitrary", "arbitrary"), vmem_limit_bytes=MLP_VMEM_LIMIT),
        name="mlp_ln2",
    )(h, w_up, w_down, lng, lnb)


def _lower_bounds(p):
    c = jnp.cumsum(jax.nn.softmax(p.astype(F32), axis=0), axis=0)
    return c - c[0:1]


def _prep_params(w_in, gla_w_lr2, gla_b_lr, gla_norm_g, hg_norm_g, lbs_f, lbs_b,
                 w_out, ln1_g, ln1_b, w_up, w_down, ln2_g, ln2_b):
    glr0 = 2 * GLA_KWIDTH + 2 * GLA_WIDTH
    w_in = w_in.astype(BF16)
    w_main = jnp.concatenate([w_in[..., :glr0], w_in[..., glr0 + 2 * GLA_RANK:]], axis=-1)
    w_glr = jnp.pad(w_in[..., glr0:glr0 + 2 * GLA_RANK], ((0, 0), (0, 0), (0, GLR_PAD - 2 * GLA_RANK)))
    stacked = dict(w_main=w_main, w_glr=w_glr, w_out=w_out.astype(BF16),
                   w_up=w_up.astype(BF16), w_down=w_down.astype(BF16))
    layers = []
    for l in range(DEPTH):
        w2 = gla_w_lr2[l].astype(F32)
        w2c = jnp.zeros((GLR_PAD, 2 * GLA_KWIDTH), F32)
        w2c = w2c.at[:GLA_RANK, :GLA_KWIDTH].set(w2[0]).at[GLA_RANK:2 * GLA_RANK, GLA_KWIDTH:].set(w2[1])
        lb = jnp.concatenate([lbs_f[l], lbs_b[l]])[None, :]
        row = lambda a: a[l].astype(F32)[None, :]
        layers.append(dict(
            w2c=w2c.astype(BF16), b2c=gla_b_lr[l].astype(F32).reshape(1, 2 * GLA_KWIDTH),
            gate_c=jnp.concatenate([lb, 1.0 - lb, jnp.log(lb), jnp.log1p(-lb),
                                    jnp.zeros((4, 2 * HG_WIDTH), F32)], axis=0),
            gng=row(gla_norm_g), hng=row(hg_norm_g), ln1_g=row(ln1_g), ln1_b=row(ln1_b),
            ln2_g=row(ln2_g), ln2_b=row(ln2_b)))
    return stacked, layers


def _trunk(x, stacked, layers):
    batch, seq, _ = x.shape
    h = x.reshape(batch * seq, D_MODEL).astype(F32)
    for l, p in enumerate(layers):
        z, logf, ga = _inproj(h, l, stacked["w_main"], stacked["w_glr"], p["w2c"], p["b2c"], p["gate_c"])
        (ofg, obg), (ofh, obh) = _scan(
            [((z, 0), (z, 1), (z, 1), (z, 1), (ga, 0), (ga, 1)),
             ((z, 3), (z, 4), (z, 5), (z, 6), (logf, 0), (logf, 1))],
            [(GLA_HEADS, GLA_DK, GLA_DV, GLA_CHUNK), (HG_HEADS, HG_DIM, HG_DIM, HG_CHUNK)],
            batch=batch, seq=seq)
        h = _outproj(ofg, obg, ofh, obh, z, h, l, stacked["w_out"], p["gng"], p["hng"], p["ln1_g"], p["ln1_b"])
        h = _mlp(h, l, stacked["w_up"], stacked["w_down"], p["ln2_g"], p["ln2_b"])
    return h.reshape(batch, seq, D_MODEL).astype(x.dtype)


def kernel(x_prompt, x_sample, w_in, gla_w_lr2, gla_b_lr, gla_norm_g, hg_norm_g, lower_bounds,
           w_out, ln1_g, ln1_b, w_up, w_down, ln2_g, ln2_b):
    lbs_f = _lower_bounds(lower_bounds[0])
    lbs_b = _lower_bounds(lower_bounds[1])
    stacked, layers = _prep_params(w_in, gla_w_lr2, gla_b_lr, gla_norm_g, hg_norm_g, lbs_f, lbs_b,
                                   w_out, ln1_g, ln1_b, w_up, w_down, ln2_g, ln2_b)
    return (_trunk(x_prompt, stacked, layers), _trunk(x_sample, stacked, layers))
```
